```python
import math
import jax, jax.numpy as jnp
from jax import lax
import numpy as np

D_MODEL = 1024
BATCH = 16
SEQ = 4096
DEPTH = 4

CHUNK = 64
MEM_LEN = 256
N_MIXERS = 3
RMS_EPS = 1e-6
GN_EPS = 1e-5
POOL_WINDOWS = (2, 4, 8, 16)
POOL_GROUPS = len(POOL_WINDOWS)
POOL_GROUP_DIM = D_MODEL // POOL_GROUPS
SB_HEADS = 16
SB_HEAD_DIM = D_MODEL // SB_HEADS
SB_BLOCK = 128
RET_HEADS = D_MODEL // 256
RET_QK_DIM = D_MODEL // RET_HEADS
RET_V_DIM = 2 * D_MODEL // RET_HEADS
ROPE_BASE = 10000.0
XA_HEADS = 4
XA_HEAD_DIM = D_MODEL // XA_HEADS
D_FF = -(-8 * D_MODEL // (3 * 256)) * 256
N_POOL_LAYERS = (DEPTH + N_MIXERS - 1) // N_MIXERS
N_SB_LAYERS = (DEPTH - 1 + N_MIXERS - 1) // N_MIXERS
N_RET_LAYERS = (DEPTH - 2 + N_MIXERS - 1) // N_MIXERS

kernel_name = "hybrid_pool_stickbreak_retention_encoder"


def rms_norm(x, g):
    xf = x.astype(jnp.float32)
    y = xf * lax.rsqrt(jnp.mean(xf * xf, axis=-1, keepdims=True) + RMS_EPS)
    return (y * g.astype(jnp.float32)).astype(x.dtype)


def pool_mixer(h, w_grp, scale):
    B, S, D = h.shape
    hg = h.reshape(B, S, POOL_GROUPS, POOL_GROUP_DIM).astype(jnp.float32)
    csum = jnp.cumsum(hg, axis=1)
    t = jnp.arange(S)
    outs = []
    for g, w in enumerate(POOL_WINDOWS):
        c = csum[:, :, g]
        lagged = jnp.pad(c, ((0, 0), (w, 0), (0, 0)))[:, :S]
        count = jnp.minimum(t + 1, w).astype(jnp.float32)[None, :, None]
        outs.append((c - lagged) / count - hg[:, :, g])
    d = jnp.stack(outs, axis=2).astype(h.dtype)
    y = jnp.einsum('bsgc,gce->bsge', d, w_grp)
    return y.reshape(B, S, D) * scale


def stick_breaking_attention(h, w_in, w_out):
    B, S, D = h.shape
    q, k, v = jnp.split(h @ w_in, 3, axis=-1)
    def heads(a):
        return a.reshape(B, S, SB_HEADS, SB_HEAD_DIM).transpose(0, 2, 1, 3).astype(jnp.float32)
    q = heads(q) * (SB_HEAD_DIM ** -0.5)
    k = heads(k)
    v = heads(v)
    outs = []
    for start in range(0, S, SB_BLOCK):
        stop = start + SB_BLOCK
        qb = q[:, :, start:stop]
        kb = k[:, :, :stop]
        vb = v[:, :, :stop]
        z = jnp.einsum('bhtd,bhsd->bhts', qb, kb)
        before = jnp.arange(stop)[None, :] < jnp.arange(start, stop)[:, None]
        log_keep = jnp.where(before, jax.nn.log_sigmoid(-z), 0.0)
        tail = lax.cumsum(log_keep, axis=3, reverse=True) - log_keep
        a = jnp.where(before, jnp.exp(jax.nn.log_sigmoid(z) + tail), 0.0)
        outs.append(jnp.einsum('bhts,bhsd->bhtd', a, vb))
    o = jnp.concatenate(outs, axis=2)
    o = o.transpose(0, 2, 1, 3).reshape(B, S, D).astype(h.dtype)
    return o @ w_out


def rotary(x, pos):
    half = x.shape[-1] // 2
    inv_freq = ROPE_BASE ** (-jnp.arange(half, dtype=jnp.float32) / half)
    ang = pos[:, None] * inv_freq[None, :]
    cos, sin = jnp.cos(ang), jnp.sin(ang)
    x1, x2 = x[..., :half], x[..., half:]
    return jnp.concatenate([x1 * cos - x2 * sin, x1 * sin + x2 * cos], axis=-1)


def retention(h, w_in, gn_gain, w_out):
    B, S, D = h.shape
    H, dk, dv = RET_HEADS, RET_QK_DIM, RET_V_DIM
    q, k, v, g = jnp.split(h @ w_in, [D, 2 * D, 4 * D], axis=-1)
    def heads(a, d):
        return a.reshape(B, S, H, d).transpose(0, 2, 1, 3).astype(jnp.float32)
    pos = jnp.arange(S, dtype=jnp.float32)
    q = rotary(heads(q, dk), pos)
    k = rotary(heads(k, dk), pos) * (dk ** -0.5)
    v = heads(v, dv)
    log_gamma = jnp.log1p(-jnp.exp2(-5.0 - jnp.arange(H, dtype=jnp.float32)))
    n = jnp.arange(CHUNK, dtype=jnp.float32)
    diff = n[:, None] - n[None, :]
    intra_decay = jnp.where(diff >= 0, jnp.exp(jnp.maximum(diff, 0.0) * log_gamma[:, None, None]), 0.0)
    query_decay = jnp.exp((n + 1.0)[None, :] * log_gamma[:, None])[..., None]
    key_decay = jnp.exp((CHUNK - 1.0 - n)[None, :] * log_gamma[:, None])[..., None]
    chunk_decay = jnp.exp(CHUNK * log_gamma)[:, None, None]
    n_chunks = S // CHUNK
    def to_chunks(a):
        return a.reshape(B, H, n_chunks, CHUNK, a.shape[-1]).transpose(2, 0, 1, 3, 4)
    def step(state, xs):
        qc, kc, vc = xs
        scores = jnp.einsum('bhcd,bhmd->bhcm', qc, kc) * intra_decay
        inner = jnp.einsum('bhcm,bhme->bhce', scores, vc)
        cross = jnp.einsum('bhcd,bhde->bhce', qc, state) * query_decay
        state = state * chunk_decay + jnp.einsum('bhmd,bhme->bhde', kc * key_decay, vc)
        return state, inner + cross
    state0 = jnp.zeros((B, H, dk, dv), jnp.float32)
    _, o = lax.scan(step, state0, (to_chunks(q), to_chunks(k), to_chunks(v)))
    o = o.transpose(1, 3, 0, 2, 4).reshape(B, S, H, dv)
    mu = jnp.mean(o, axis=-1, keepdims=True)
    var = jnp.mean(jnp.square(o - mu), axis=-1, keepdims=True)
    o = ((o - mu) * lax.rsqrt(var + GN_EPS)).reshape(B, S, H * dv) * gn_gain.astype(jnp.float32)
    y = jax.nn.silu(g.astype(jnp.float32)) * o
    return y.astype(h.dtype) @ w_out


def memory_cross_attention(h, mem_n, w_q, w_kv, w_o):
    B, S, D = h.shape
    q = (h @ w_q).reshape(B, S, XA_HEADS, XA_HEAD_DIM)
    k, v = jnp.split(mem_n @ w_kv, 2, axis=-1)
    k = k.reshape(B, -1, XA_HEADS, XA_HEAD_DIM)
    v = v.reshape(B, -1, XA_HEADS, XA_HEAD_DIM)
    s = jnp.einsum('bshd,bmhd->bhsm', q, k).astype(jnp.float32) * (XA_HEAD_DIM ** -0.5)
    p = jax.nn.softmax(s, axis=-1).astype(h.dtype)
    o = jnp.einsum('bhsm,bmhd->bshd', p, v).reshape(B, S, D)
    return o @ w_o


def swiglu(h, w_in, w_out):
    gate, up = jnp.split(h @ w_in, 2, axis=-1)
    return (jax.nn.silu(gate) * up) @ w_out


def setup_inputs(seed: int = 0) -> dict:
    key = jax.random.key(seed)
    ks = jax.random.split(key, 24)
    f32 = jnp.float32
    def dense(k, shape, fan_in):
        return jax.random.normal(k, shape, f32) * (fan_in ** -0.5)
    def gain(k, shape):
        return 1.0 + 0.1 * jax.random.normal(k, shape, f32)
    D = D_MODEL
    return {
        "x": jax.random.normal(ks[0], (BATCH, SEQ, D), f32),
        "mem": jax.random.normal(ks[1], (BATCH, MEM_LEN, D), f32),
        "norm_mix_pre": gain(ks[2], (DEPTH, D)),
        "norm_mix_post": gain(ks[3], (DEPTH, D)),
        "norm_xa_pre": gain(ks[4], (DEPTH, D)),
        "norm_xa_post": gain(ks[5], (DEPTH, D)),
        "norm_mem": gain(ks[6], (DEPTH, D)),
        "norm_ffn_pre": gain(ks[7], (DEPTH, D)),
        "norm_ffn_post": gain(ks[8], (DEPTH, D)),
        "pool_w": dense(ks[9], (N_POOL_LAYERS, POOL_GROUPS, POOL_GROUP_DIM, POOL_GROUP_DIM), POOL_GROUP_DIM),
        "pool_scale": gain(ks[10], (N_POOL_LAYERS, D)),
        "sb_w_in": dense(ks[11], (N_SB_LAYERS, D, 3 * D), D),
        "sb_w_out": dense(ks[12], (N_SB_LAYERS, D, D), D),
        "ret_w_in": dense(ks[13], (N_RET_LAYERS, D, 6 * D), D),
        "ret_gn": gain(ks[14], (N_RET_LAYERS, 2 * D)),
        "ret_w_out": dense(ks[15], (N_RET_LAYERS, 2 * D, D), 2 * D),
        "xa_w_q": dense(ks[16], (DEPTH, D, D), D),
        "xa_w_kv": dense(ks[17], (DEPTH, D, 2 * D), D),
        "xa_w_o": dense(ks[18], (DEPTH, D, D), D),
        "ffn_w_in": dense(ks[19], (DEPTH, D, 2 * D_FF), D),
        "ffn_w_out": dense(ks[20], (DEPTH, D_FF, D), D_FF),
    }


def reference(x, mem, norm_mix_pre, norm_mix_post, norm_xa_pre, norm_xa_post, norm_mem,
              norm_ffn_pre, norm_ffn_post, pool_w, pool_scale, sb_w_in, sb_w_out,
              ret_w_in, ret_gn, ret_w_out, xa_w_q, xa_w_kv, xa_w_o, ffn_w_in, ffn_w_out):
    for i in range(DEPTH):
        kind = i % N_MIXERS
        j = i // N_MIXERS
        hn = rms_norm(x, norm_mix_pre[i])
        if kind == 0:
            m = pool_mixer(hn, pool_w[j], pool_scale[j])
        elif kind == 1:
            m = stick_breaking_attention(hn, sb_w_in[j], sb_w_out[j])
        else:
            m = retention(hn, ret_w_in[j], ret_gn[j], ret_w_out[j])
        x = x + rms_norm(m, norm_mix_post[i])
        mem_n = rms_norm(mem, norm_mem[i])
        c = memory_cross_attention(rms_norm(x, norm_xa_pre[i]), mem_n, xa_w_q[i], xa_w_kv[i], xa_w_o[i])
        x = x + rms_norm(c, norm_xa_post[i])
        f = swiglu(rms_norm(x, norm_ffn_pre[i]), ffn_w_in[i], ffn_w_out[i])
        x = x + rms_norm(f, norm_ffn_post[i])
    return x
```

```python
import functools
import math

import jax
import jax.numpy as jnp
import numpy as np
from jax import lax
from jax.experimental import pallas as pl
from jax.experimental.pallas import tpu as pltpu

N_MIXERS = 3
RMS_EPS = 1e-6
GN_EPS = 1e-5
POOL_WINDOWS = (2, 4, 8, 16)
POOL_HALO = 16
SB_HEADS = 16
SB_HEAD_DIM = 64
RET_HEADS = 4
ROPE_BASE = 10000.0
XA_HEADS = 4

LANES = 128
VMEM_LIMIT_BYTES = 56 * 1024 * 1024

TOKEN_TILE = 256
SB_BLOCK = 128
RET_REF_CHUNK = 64
RET_CHUNK = 256
RET_STEP = 512
SB_LOG_ZERO = -104.0

BF16 = jnp.bfloat16
F32 = jnp.float32


def _params(*sem):
    return pltpu.CompilerParams(dimension_semantics=sem, vmem_limit_bytes=VMEM_LIMIT_BYTES)


def _const_spec(shape, index_map):
    return pl.BlockSpec(shape, index_map, pipeline_mode=pl.Buffered(1))


def _rms(x, g):
    ms = jnp.mean(x * x, axis=-1, keepdims=True)
    return x * lax.rsqrt(ms + RMS_EPS) * g


def _dot(a, b):
    return jnp.dot(a, b, preferred_element_type=F32)


def _dot_nt(a, b):
    return lax.dot_general(a, b, (((1,), (1,)), ((), ())), preferred_element_type=F32)


def _sigmoid(x):
    return 1.0 / (1.0 + jnp.exp(-x))


def _kv_body(mem_ref, g_ref, w_ref, k_ref, v_ref):
    d = mem_ref.shape[-1]
    mn = _rms(mem_ref[0], g_ref[0]).astype(BF16)
    k_ref[0, 0] = _dot(mn, w_ref[0, :, :d]).astype(BF16)
    v_ref[0, 0] = _dot(mn, w_ref[0, :, d:]).astype(BF16)


def _memory_kv(mem, norm_mem, w_kv):
    b, m, d = mem.shape
    depth = w_kv.shape[0]
    out = jax.ShapeDtypeStruct((depth, b, m, d), BF16)
    return pl.pallas_call(
        _kv_body,
        grid=(depth, b),
        in_specs=[
            pl.BlockSpec((1, m, d), lambda i, j: (j, 0, 0)),
            pl.BlockSpec((1, 1, d), lambda i, j: (i, 0, 0)),
            pl.BlockSpec((1, d, 2 * d), lambda i, j: (i, 0, 0)),
        ],
        out_specs=[pl.BlockSpec((1, 1, m, d), lambda i, j: (i, j, 0, 0))] * 2,
        out_shape=[out, out],
        compiler_params=_params("arbitrary", "arbitrary"),
        name="memory_kv",
    )(mem, norm_mem.reshape(depth, 1, d), w_kv)


def _post_body(has_wmix, x_ref, m_ref, *refs):
    if has_wmix:
        wmix_ref, refs = refs[0], refs[1:]
    (g_mix, g_xa_pre, g_xa_post, g_ffn_pre, g_ffn_post, k_ref, v_ref,
     wq_ref, wo_ref, win_ref, wout_ref, o_ref) = refs
    d = x_ref.shape[-1]
    x = x_ref[...]

    m = _dot(m_ref[...].astype(BF16), wmix_ref[0]) if has_wmix else m_ref[...]
    x = x + _rms(m, g_mix[0])

    hd = d // XA_HEADS
    h = _rms(x, g_xa_pre[0]).astype(BF16)
    q = _dot(h, wq_ref[0]).astype(BF16)
    c = None
    for hh in range(XA_HEADS):
        sl = slice(hh * hd, (hh + 1) * hd)
        s = _dot_nt(q[:, sl], k_ref[0, 0, :, sl]) * (hd ** -0.5)
        e = jnp.exp(s - jnp.max(s, axis=-1, keepdims=True))
        p = (e * (1.0 / jnp.sum(e, axis=-1, keepdims=True))).astype(BF16)
        oh = _dot(p, v_ref[0, 0, :, sl]).astype(BF16)
        ch = _dot(oh, wo_ref[0, sl, :])
        c = ch if c is None else c + ch
    x = x + _rms(c, g_xa_post[0])

    dff = wout_ref.shape[1]
    fc = 2 * LANES
    h = _rms(x, g_ffn_pre[0]).astype(BF16)
    f = None
    for c0 in range(0, dff, fc):
        gate = _dot(h, win_ref[0, :, c0:c0 + fc])
        up = _dot(h, win_ref[0, :, dff + c0:dff + c0 + fc])
        act = (gate * _sigmoid(gate) * up).astype(BF16)
        fch = _dot(act, wout_ref[0, c0:c0 + fc, :])
        f = fch if f is None else f + fch
    o_ref[...] = x + _rms(f, g_ffn_post[0])


def _post_layer(layer, x, m, w_mix, gains, kmem, vmem, w_q, w_o, w_in, w_out, seq):
    t, d = x.shape
    tm = TOKEN_TILE
    tiles_per_seq = seq // tm
    dff = w_out.shape[1]
    mem_len = kmem.shape[2]
    row = lambda i: (i, 0)
    lay = lambda i: (layer, 0, 0)
    in_specs = [pl.BlockSpec((tm, d), row), pl.BlockSpec((tm, m.shape[1]), row)]
    args = [x, m]
    if w_mix is not None:
        j, w_stack = w_mix
        in_specs.append(_const_spec((1,) + w_stack.shape[1:], lambda i: (j, 0, 0)))
        args.append(w_stack)
    for g in gains:
        in_specs.append(pl.BlockSpec((1, 1, d), lay))
        args.append(g)
    kv_spec = pl.BlockSpec((1, 1, mem_len, d), lambda i: (layer, i // tiles_per_seq, 0, 0))
    in_specs += [
        kv_spec, kv_spec,
        _const_spec((1, d, d), lay), _const_spec((1, d, d), lay),
        _const_spec((1, d, 2 * dff), lay), _const_spec((1, dff, d), lay),
    ]
    args += [kmem, vmem, w_q, w_o, w_in, w_out]
    return pl.pallas_call(
        functools.partial(_post_body, w_mix is not None),
        grid=(t // tm,),
        in_specs=in_specs,
        out_specs=pl.BlockSpec((tm, d), row),
        out_shape=jax.ShapeDtypeStruct((t, d), F32),
        compiler_params=_params("arbitrary"),
        name="post_mixer",
    )(*args)


def _pool_body(x_ref, g_ref, w_ref, scale_ref, o_ref, hbuf):
    tm, d = x_ref.shape[1], x_ref.shape[2]
    gd = d // len(POOL_WINDOWS)
    s_idx = pl.program_id(1)

    @pl.when(s_idx == 0)
    def _():
        hbuf[0:POOL_HALO, :] = jnp.zeros((POOL_HALO, d), F32)

    @pl.when(s_idx != 0)
    def _():
        hbuf[0:POOL_HALO, :] = hbuf[tm:tm + POOL_HALO, :]

    hn = _rms(x_ref[0], g_ref[...])
    hbuf[POOL_HALO:POOL_HALO + tm, :] = hn

    t = s_idx * tm + lax.broadcasted_iota(jnp.int32, (tm, 1), 0)
    for g, w in enumerate(POOL_WINDOWS):
        cols = slice(g * gd, (g + 1) * gd)
        acc = hn[:, cols]
        for i in range(1, w):
            acc = acc + hbuf[POOL_HALO - i:POOL_HALO - i + tm, cols]
        inv = 1.0 / jnp.minimum(t + 1, w).astype(F32)
        dlt = (acc * inv - hn[:, cols]).astype(BF16)
        o_ref[0, :, cols] = _dot(dlt, w_ref[0, g]) * scale_ref[0, :, cols]


def _pool_mixer(j, x3, g_pre, pool_w, pool_scale):
    b, s, d = x3.shape
    tm = TOKEN_TILE
    ng, gd = pool_w.shape[1], pool_w.shape[2]
    return pl.pallas_call(
        _pool_body,
        grid=(b, s // tm),
        in_specs=[
            pl.BlockSpec((1, tm, d), lambda i, k: (i, k, 0)),
            pl.BlockSpec((1, d), lambda i, k: (0, 0)),
            pl.BlockSpec((1, ng, gd, gd), lambda i, k: (j, 0, 0, 0)),
            pl.BlockSpec((1, 1, d), lambda i, k: (j, 0, 0)),
        ],
        out_specs=pl.BlockSpec((1, tm, d), lambda i, k: (i, k, 0)),
        out_shape=jax.ShapeDtypeStruct((b, s, d), F32),
        scratch_shapes=[pltpu.VMEM((tm + POOL_HALO, d), F32)],
        compiler_params=_params("arbitrary", "arbitrary"),
        name="pool_mixer",
    )(x3, g_pre, pool_w, pool_scale)


def _sb_proj_body(x_ref, g_ref, w_ref, q_ref, k_ref, v_ref):
    d = x_ref.shape[-1]
    hn = _rms(x_ref[...], g_ref[...]).astype(BF16)
    q_ref[...] = (_dot(hn, w_ref[0, :, :d]) * (SB_HEAD_DIM ** -0.5)).astype(BF16)
    k_ref[...] = _dot(hn, w_ref[0, :, d:2 * d]).astype(BF16)
    v_ref[...] = _dot(hn, w_ref[0, :, 2 * d:]).astype(BF16)


def _sb_project(j, x, g_pre, w_in):
    t, d = x.shape
    tm = TOKEN_TILE
    row = lambda i: (i, 0)
    out = jax.ShapeDtypeStruct((t, d), BF16)
    return pl.pallas_call(
        _sb_proj_body,
        grid=(t // tm,),
        in_specs=[
            pl.BlockSpec((tm, d), row),
            pl.BlockSpec((1, d), lambda i: (0, 0)),
            _const_spec((1, d, 3 * d), lambda i: (j, 0, 0)),
        ],
        out_specs=[pl.BlockSpec((tm, d), row)] * 3,
        out_shape=[out, out, out],
        compiler_params=_params("arbitrary"),
        name="sb_project",
    )(x, g_pre, w_in)


def _sb_attn_body(q_ref, k_ref, v_ref, cum_ref, o_ref):
    tq = q_ref.shape[1]
    i = pl.program_id(2)
    q = q_ref[0]
    lane = lax.broadcasted_iota(jnp.int32, (tq, LANES), 1)
    row = lax.broadcasted_iota(jnp.int32, (tq, SB_BLOCK), 0)
    col = lax.broadcasted_iota(jnp.int32, (tq, SB_BLOCK), 1)
    cum = cum_ref[...]

    def one_head(qh):
        def cond(carry):
            j, r, _ = carry
            return jnp.logical_and(j >= 0, jnp.max(r) > SB_LOG_ZERO)

        def body(carry):
            j, r, acc = carry
            start = pl.multiple_of(j * SB_BLOCK, SB_BLOCK)
            kj = k_ref[0, pl.ds(start, SB_BLOCK), :]
            vj = v_ref[0, pl.ds(start, SB_BLOCK), :]
            z = _dot_nt(qh, kj)
            before = (col + (j - i) * SB_BLOCK) < row
            log_keep = -(jnp.maximum(z, 0.0) + jnp.log1p(jnp.exp(-jnp.abs(z))))
            lk = jnp.where(before, log_keep, 0.0)
            hi = lk.astype(BF16)
            lo = (lk - hi.astype(F32)).astype(BF16)
            t = _dot(jnp.concatenate([hi, lo], axis=1), cum)
            tail = t[:, :SB_BLOCK] + r
            a = jnp.where(before, jnp.exp(z + log_keep + tail), 0.0)
            acc = acc + _dot(a.astype(BF16), vj)
            return j - 1, r + t[:, SB_BLOCK:], acc

        zeros = jnp.zeros((tq, LANES), F32)
        _, _, acc = lax.while_loop(cond, body, (i, zeros, zeros))
        return acc

    first = lane < SB_HEAD_DIM
    zero = jnp.zeros_like(q)
    o0 = one_head(jnp.where(first, q, zero))
    o1 = one_head(jnp.where(first, zero, q))
    o_ref[0] = jnp.where(first, o0, o1).astype(BF16)


def _sb_cumsum_matrix():
    n = SB_BLOCK
    tri = (np.arange(n)[:, None] > np.arange(n)[None, :]).astype(np.float32)
    half = np.concatenate([tri, np.ones((n, n), np.float32)], axis=1)
    return jnp.asarray(np.concatenate([half, half], axis=0), dtype=BF16)


def _sb_attention(q, k, v):
    b, s, d = q.shape
    tq = SB_BLOCK
    blk = lambda bi, hp, i: (bi, i, hp)
    full = lambda bi, hp, i: (bi, 0, hp)
    return pl.pallas_call(
        _sb_attn_body,
        grid=(b, d // LANES, s // tq),
        in_specs=[
            pl.BlockSpec((1, tq, LANES), blk),
            pl.BlockSpec((1, s, LANES), full),
            pl.BlockSpec((1, s, LANES), full),
            pl.BlockSpec((2 * SB_BLOCK, 2 * SB_BLOCK), lambda bi, hp, i: (0, 0)),
        ],
        out_specs=pl.BlockSpec((1, tq, LANES), blk),
        out_shape=jax.ShapeDtypeStruct((b, s, d), BF16),
        compiler_params=_params("arbitrary", "arbitrary", "arbitrary"),
        name="sb_attention",
    )(q, k, v, _sb_cumsum_matrix())


def _ret_proj_body(x_ref, g_ref, w_ref, wkt_ref, cos_ref, sin_ref, cost_ref, sint_ref,
                   q_ref, kt_ref, v_ref, gate_ref):
    d = x_ref.shape[-1]
    dk = d // RET_HEADS
    half = dk // 2
    hn = _rms(x_ref[...], g_ref[...]).astype(BF16)
    cos, sin = cos_ref[...], sin_ref[...]
    cos_t, sin_t = cost_ref[...], sint_ref[...]
    for h in range(RET_HEADS):
        a = slice(h * dk, h * dk + half)
        bsl = slice(h * dk + half, (h + 1) * dk)
        x1 = _dot(hn, w_ref[0, :, a])
        x2 = _dot(hn, w_ref[0, :, bsl])
        q_ref[:, a] = (x1 * cos - x2 * sin).astype(BF16)
        q_ref[:, bsl] = (x1 * sin + x2 * cos).astype(BF16)
        y1 = _dot_nt(wkt_ref[0, a, :], hn)
        y2 = _dot_nt(wkt_ref[0, bsl, :], hn)
        kt_ref[0, a, :] = ((y1 * cos_t - y2 * sin_t) * (dk ** -0.5)).astype(BF16)
        kt_ref[0, bsl, :] = ((y1 * sin_t + y2 * cos_t) * (dk ** -0.5)).astype(BF16)
    for c0 in range(0, 2 * d, 4 * LANES):
        v_ref[:, c0:c0 + 4 * LANES] = _dot(hn, w_ref[0, :, 2 * d + c0:2 * d + c0 + 4 * LANES]).astype(BF16)
        gate_ref[:, c0:c0 + 4 * LANES] = _dot(hn, w_ref[0, :, 4 * d + c0:4 * d + c0 + 4 * LANES])


def _ret_project(j, x, g_pre, w_in, w_kt, seq):
    t, d = x.shape
    tm = TOKEN_TILE
    b = t // seq
    tiles_per_seq = seq // tm
    half = d // RET_HEADS // 2
    pos = jnp.arange(seq, dtype=F32)
    inv_freq = ROPE_BASE ** (-jnp.arange(half, dtype=F32) / half)
    ang = pos[:, None] * inv_freq[None, :]
    cos, sin = jnp.cos(ang), jnp.sin(ang)
    row = lambda i: (i, 0)
    tab = pl.BlockSpec((tm, half), lambda i: (i % tiles_per_seq, 0))
    tab_t = pl.BlockSpec((half, tm), lambda i: (0, i % tiles_per_seq))
    return pl.pallas_call(
        _ret_proj_body,
        grid=(t // tm,),
        in_specs=[
            pl.BlockSpec((tm, d), row),
            pl.BlockSpec((1, d), lambda i: (0, 0)),
            _const_spec((1, d, 6 * d), lambda i: (j, 0, 0)),
            _const_spec((1, d, d), lambda i: (j, 0, 0)),
            tab, tab, tab_t, tab_t,
        ],
        out_specs=[
            pl.BlockSpec((tm, d), row),
            pl.BlockSpec((1, d, tm), lambda i: (i // tiles_per_seq, 0, i % tiles_per_seq)),
            pl.BlockSpec((tm, 2 * d), row),
            pl.BlockSpec((tm, 2 * d), row),
        ],
        out_shape=[
            jax.ShapeDtypeStruct((t, d), BF16),
            jax.ShapeDtypeStruct((b, d, seq), BF16),
            jax.ShapeDtypeStruct((t, 2 * d), BF16),
            jax.ShapeDtypeStruct((t, 2 * d), F32),
        ],
        compiler_params=_params("arbitrary"),
        name="ret_project",
    )(x, g_pre, w_in, w_kt, cos, sin, cos.T, sin.T)


def _ret_body(q_ref, kt_ref, v_ref, gate_ref, gn_ref, intra_ref, qdec_ref, kdec_ref, cdec_ref,
              o_ref, state):
    @pl.when(pl.program_id(2) == 0)
    def _():
        state[...] = jnp.zeros_like(state)

    per_chunk = RET_CHUNK // RET_REF_CHUNK
    for sub in range(RET_STEP // RET_CHUNK):
        rows = slice(sub * RET_CHUNK, (sub + 1) * RET_CHUNK)
        q = q_ref[0, rows, :]
        kt = kt_ref[0, :, rows]
        v = v_ref[0, rows, :]
        scores = (_dot(q, kt) * intra_ref[0]).astype(BF16)
        o = _dot(scores, v) + _dot(q, state[...].astype(BF16)) * qdec_ref[0]
        kt_dec = (kt.astype(F32) * kdec_ref[0]).astype(BF16)
        state[...] = state[...] * cdec_ref[0] + _dot(kt_dec, v)

        mu = jnp.mean(o, axis=-1, keepdims=True)
        oc = o - mu
        var = jnp.mean(oc * oc, axis=-1, keepdims=True)
        on = oc * lax.rsqrt(var + GN_EPS) * gn_ref[...]
        for cl in range(per_chunk):
            c = sub * per_chunk + cl
            g = gate_ref[0, :, c, :]
            o_ref[0, :, c, :] = g * _sigmoid(g) * on[cl * RET_REF_CHUNK:(cl + 1) * RET_REF_CHUNK, :]


def _ret_decay_tables(chunk):
    h = jnp.arange(RET_HEADS, dtype=F32)
    log_gamma = jnp.log1p(-jnp.exp2(-5.0 - h))
    n = jnp.arange(chunk, dtype=F32)
    diff = n[:, None] - n[None, :]
    intra = jnp.where(diff >= 0, jnp.exp(jnp.maximum(diff, 0.0) * log_gamma[:, None, None]), 0.0)
    qdec = jnp.exp((n + 1.0)[None, :] * log_gamma[:, None])[..., None]
    kdec = jnp.exp((chunk - 1.0 - n)[None, :] * log_gamma[:, None])[:, None]
    cdec = jnp.exp(chunk * log_gamma)[:, None, None]
    return intra, qdec, kdec, cdec


def _retention(q, kt, v, gate, gn):
    b, s, d = q.shape
    c = RET_CHUNK
    step = RET_STEP
    dk = d // RET_HEADS
    dv = 2 * d // RET_HEADS
    n_ref_chunks = s // RET_REF_CHUNK
    ref_chunks_per_step = step // RET_REF_CHUNK
    intra, qdec, kdec, cdec = _ret_decay_tables(c)
    head = lambda bi, h, ci: (h, 0, 0)
    permuted = pl.BlockSpec((1, RET_REF_CHUNK, ref_chunks_per_step, dv), lambda bi, h, ci: (bi, 0, ci, h))
    out = pl.pallas_call(
        _ret_body,
        grid=(b, RET_HEADS, s // step),
        in_specs=[
            pl.BlockSpec((1, step, dk), lambda bi, h, ci: (bi, ci, h)),
            pl.BlockSpec((1, dk, step), lambda bi, h, ci: (bi, h, ci)),
            pl.BlockSpec((1, step, dv), lambda bi, h, ci: (bi, ci, h)),
            permuted,
            pl.BlockSpec((1, dv), lambda bi, h, ci: (0, h)),
            pl.BlockSpec((1, c, c), head),
            pl.BlockSpec((1, c, 1), head),
            pl.BlockSpec((1, 1, c), head),
            pl.BlockSpec((1, 1, 1), head),
        ],
        out_specs=permuted,
        out_shape=jax.ShapeDtypeStruct((b, RET_REF_CHUNK, n_ref_chunks, 2 * d), F32),
        scratch_shapes=[pltpu.VMEM((dk, dv), F32)],
        compiler_params=_params("arbitrary", "arbitrary", "arbitrary"),
        name="retention",
    )(q, kt, v, gate.reshape(b, RET_REF_CHUNK, n_ref_chunks, 2 * d), gn, intra, qdec, kdec, cdec)
    return out.reshape(b * s, 2 * d)


def kernel(x, mem, norm_mix_pre, norm_mix_post, norm_xa_pre, norm_xa_post, norm_mem, norm_ffn_pre, norm_ffn_post, pool_w, pool_scale, sb_w_in, sb_w_out, ret_w_in, ret_gn, ret_w_out, xa_w_q, xa_w_kv, xa_w_o, ffn_w_in, ffn_w_out):
    b, s, d = x.shape
    depth = norm_mix_pre.shape[0]
    t = b * s
    assert s % TOKEN_TILE == 0 and s % RET_STEP == 0 and s % SB_BLOCK == 0

    bf = lambda w: w.astype(BF16)
    pool_w, sb_w_in, sb_w_out, ret_w_in, ret_w_out = map(bf, (pool_w, sb_w_in, sb_w_out, ret_w_in, ret_w_out))
    xa_w_q, xa_w_kv, xa_w_o, ffn_w_in, ffn_w_out = map(bf, (xa_w_q, xa_w_kv, xa_w_o, ffn_w_in, ffn_w_out))
    ret_w_kt = jnp.swapaxes(ret_w_in[:, :, d:2 * d], 1, 2)
    gain3 = lambda g: g.reshape(depth, 1, d)
    post_gains = tuple(map(gain3, (norm_mix_post, norm_xa_pre, norm_xa_post, norm_ffn_pre, norm_ffn_post)))

    kmem, vmem = _memory_kv(mem, norm_mem, xa_w_kv)

    xf = x.reshape(t, d)
    for i in range(depth):
        kind, j = i % N_MIXERS, i // N_MIXERS
        g_pre = norm_mix_pre[i].reshape(1, d)
        w_mix = None
        if kind == 0:
            m = _pool_mixer(j, xf.reshape(b, s, d), g_pre, pool_w, pool_scale.reshape(-1, 1, d)).reshape(t, d)
        elif kind == 1:
            q, k, v = _sb_project(j, xf, g_pre, sb_w_in)
            shape3 = lambda a: a.reshape(b, s, d)
            m = _sb_attention(shape3(q), shape3(k), shape3(v)).reshape(t, d)
            w_mix = (j, sb_w_out)
        else:
            q, kt, v, gate = _ret_project(j, xf, g_pre, ret_w_in, ret_w_kt, s)
            m = _retention(q.reshape(b, s, d), kt, v.reshape(b, s, 2 * d), gate, ret_gn[j].reshape(1, 2 * d))
            w_mix = (j, ret_w_out)
        xf = _post_layer(i, xf, m, w_mix, post_gains, kmem, vmem, xa_w_q, xa_w_o, ffn_w_in, ffn_w_out, s)
    return xf.reshape(b, s, d)
```

```python
import functools
import math

import jax
import jax.numpy as jnp
import numpy as np
from jax import lax
from jax.experimental import pallas as pl
from jax.experimental.pallas import tpu as pltpu

N_MIXERS = 3
RMS_EPS = 1e-6
GN_EPS = 1e-5
POOL_WINDOWS = (2, 4, 8, 16)
POOL_HALO = 16
SB_HEADS = 16
SB_HEAD_DIM = 64
RET_HEADS = 4
ROPE_BASE = 10000.0
XA_HEADS = 4

LANES = 128
VMEM_LIMIT_BYTES = 56 * 1024 * 1024

TOKEN_TILE = 256
SB_BLOCK = 128
RET_REF_CHUNK = 64
RET_CHUNK = 256
RET_STEP = 512
SB_PAIRS_PER_STEP = 4
SB_STATIC_BLOCKS = 3
SB_LOG_ZERO = -87.5

BF16 = jnp.bfloat16
F32 = jnp.float32


def _params(*sem):
    return pltpu.CompilerParams(dimension_semantics=sem, vmem_limit_bytes=VMEM_LIMIT_BYTES)


def _const_spec(shape, index_map):
    return pl.BlockSpec(shape, index_map, pipeline_mode=pl.Buffered(1))


def _rms(x, g):
    ms = jnp.mean(x * x, axis=-1, keepdims=True)
    return x * lax.rsqrt(ms + RMS_EPS) * g


def _dot(a, b):
    return jnp.dot(a, b, preferred_element_type=F32)


def _dot_nt(a, b):
    return lax.dot_general(a, b, (((1,), (1,)), ((), ())), preferred_element_type=F32)


def _sigmoid(x):
    return 1.0 / (1.0 + jnp.exp(-x))


def _kv_body(mem_ref, g_ref, w_ref, k_ref, v_ref):
    d = mem_ref.shape[-1]
    mn = _rms(mem_ref[0], g_ref[0]).astype(BF16)
    k_ref[0, 0] = _dot(mn, w_ref[0, :, :d]).astype(BF16)
    v_ref[0, 0] = _dot(mn, w_ref[0, :, d:]).astype(BF16)


def _memory_kv(mem, norm_mem, w_kv):
    b, m, d = mem.shape
    depth = w_kv.shape[0]
    out = jax.ShapeDtypeStruct((depth, b, m, d), BF16)
    return pl.pallas_call(
        _kv_body,
        grid=(depth, b),
        in_specs=[
            pl.BlockSpec((1, m, d), lambda i, j: (j, 0, 0)),
            pl.BlockSpec((1, 1, d), lambda i, j: (i, 0, 0)),
            pl.BlockSpec((1, d, 2 * d), lambda i, j: (i, 0, 0)),
        ],
        out_specs=[pl.BlockSpec((1, 1, m, d), lambda i, j: (i, j, 0, 0))] * 2,
        out_shape=[out, out],
        compiler_params=_params("arbitrary", "arbitrary"),
        name="memory_kv",
    )(mem, norm_mem.reshape(depth, 1, d), w_kv)


def _post_body(has_wmix, x_ref, m_ref, *refs):
    if has_wmix:
        wmix_ref, refs = refs[0], refs[1:]
    (g_mix, g_xa_pre, g_xa_post, g_ffn_pre, g_ffn_post, k_ref, v_ref,
     wq_ref, wo_ref, win_ref, wout_ref, o_ref) = refs
    d = x_ref.shape[-1]
    x = x_ref[...]

    m = _dot(m_ref[...].astype(BF16), wmix_ref[0]) if has_wmix else m_ref[...]
    x = x + _rms(m, g_mix[0])

    hd = d // XA_HEADS
    h = _rms(x, g_xa_pre[0]).astype(BF16)
    q = _dot(h, wq_ref[0]).astype(BF16)
    heads = [slice(hh * hd, (hh + 1) * hd) for hh in range(XA_HEADS)]
    scores = [_dot_nt(q[:, sl], k_ref[0, 0, :, sl]) * (hd ** -0.5) for sl in heads]
    probs = []
    for s in scores:
        e = jnp.exp(s - jnp.max(s, axis=-1, keepdims=True))
        probs.append((e * (1.0 / jnp.sum(e, axis=-1, keepdims=True))).astype(BF16))
    outs = [_dot(p, v_ref[0, 0, :, sl]).astype(BF16) for p, sl in zip(probs, heads)]
    c = functools.reduce(lambda a, b: a + b, [_dot(oh, wo_ref[0, sl, :]) for oh, sl in zip(outs, heads)])
    x = x + _rms(c, g_xa_post[0])

    dff = wout_ref.shape[1]
    fc = 2 * LANES
    h = _rms(x, g_ffn_pre[0]).astype(BF16)
    f = None
    for c0 in range(0, dff, fc):
        gate = _dot(h, win_ref[0, :, c0:c0 + fc])
        up = _dot(h, win_ref[0, :, dff + c0:dff + c0 + fc])
        act = (gate * _sigmoid(gate) * up).astype(BF16)
        fch = _dot(act, wout_ref[0, c0:c0 + fc, :])
        f = fch if f is None else f + fch
    o_ref[...] = x + _rms(f, g_ffn_post[0])


def _post_layer(layer, x, m, w_mix, gains, kmem, vmem, w_q, w_o, w_in, w_out, seq):
    t, d = x.shape
    tm = TOKEN_TILE
    tiles_per_seq = seq // tm
    dff = w_out.shape[1]
    mem_len = kmem.shape[2]
    row = lambda i: (i, 0)
    lay = lambda i: (layer, 0, 0)
    in_specs = [pl.BlockSpec((tm, d), row), pl.BlockSpec((tm, m.shape[1]), row)]
    args = [x, m]
    if w_mix is not None:
        j, w_stack = w_mix
        in_specs.append(_const_spec((1,) + w_stack.shape[1:], lambda i: (j, 0, 0)))
        args.append(w_stack)
    for g in gains:
        in_specs.append(pl.BlockSpec((1, 1, d), lay))
        args.append(g)
    kv_spec = pl.BlockSpec((1, 1, mem_len, d), lambda i: (layer, i // tiles_per_seq, 0, 0))
    in_specs += [
        kv_spec, kv_spec,
        _const_spec((1, d, d), lay), _const_spec((1, d, d), lay),
        _const_spec((1, d, 2 * dff), lay), _const_spec((1, dff, d), lay),
    ]
    args += [kmem, vmem, w_q, w_o, w_in, w_out]
    return pl.pallas_call(
        functools.partial(_post_body, w_mix is not None),
        grid=(t // tm,),
        in_specs=in_specs,
        out_specs=pl.BlockSpec((tm, d), row),
        out_shape=jax.ShapeDtypeStruct((t, d), F32),
        compiler_params=_params("arbitrary"),
        name="post_mixer",
    )(*args)


def _pool_body(x_ref, g_ref, w_ref, scale_ref, o_ref, hbuf):
    tm, d = x_ref.shape[1], x_ref.shape[2]
    gd = d // len(POOL_WINDOWS)
    s_idx = pl.program_id(1)

    @pl.when(s_idx == 0)
    def _():
        hbuf[0:POOL_HALO, :] = jnp.zeros((POOL_HALO, d), F32)

    @pl.when(s_idx != 0)
    def _():
        hbuf[0:POOL_HALO, :] = hbuf[tm:tm + POOL_HALO, :]

    hn = _rms(x_ref[0], g_ref[...])
    hbuf[POOL_HALO:POOL_HALO + tm, :] = hn

    t = s_idx * tm + lax.broadcasted_iota(jnp.int32, (tm, 1), 0)
    for g, w in enumerate(POOL_WINDOWS):
        cols = slice(g * gd, (g + 1) * gd)
        acc = hn[:, cols]
        for i in range(1, w):
            acc = acc + hbuf[POOL_HALO - i:POOL_HALO - i + tm, cols]
        inv = 1.0 / jnp.minimum(t + 1, w).astype(F32)
        dlt = (acc * inv - hn[:, cols]).astype(BF16)
        o_ref[0, :, cols] = _dot(dlt, w_ref[0, g]) * scale_ref[0, :, cols]


def _pool_mixer(j, x3, g_pre, pool_w, pool_scale):
    b, s, d = x3.shape
    tm = TOKEN_TILE
    ng, gd = pool_w.shape[1], pool_w.shape[2]
    return pl.pallas_call(
        _pool_body,
        grid=(b, s // tm),
        in_specs=[
            pl.BlockSpec((1, tm, d), lambda i, k: (i, k, 0)),
            pl.BlockSpec((1, d), lambda i, k: (0, 0)),
            pl.BlockSpec((1, ng, gd, gd), lambda i, k: (j, 0, 0, 0)),
            pl.BlockSpec((1, 1, d), lambda i, k: (j, 0, 0)),
        ],
        out_specs=pl.BlockSpec((1, tm, d), lambda i, k: (i, k, 0)),
        out_shape=jax.ShapeDtypeStruct((b, s, d), F32),
        scratch_shapes=[pltpu.VMEM((tm + POOL_HALO, d), F32)],
        compiler_params=_params("arbitrary", "arbitrary"),
        name="pool_mixer",
    )(x3, g_pre, pool_w, pool_scale)


def _sb_proj_body(x_ref, g_ref, w_ref, q_ref, k_ref, v_ref):
    d = x_ref.shape[-1]
    hn = _rms(x_ref[...], g_ref[...]).astype(BF16)
    q_ref[...] = (_dot(hn, w_ref[0, :, :d]) * (SB_HEAD_DIM ** -0.5)).astype(BF16)
    k_ref[...] = _dot(hn, w_ref[0, :, d:2 * d]).astype(BF16)
    v_ref[...] = _dot(hn, w_ref[0, :, 2 * d:]).astype(BF16)


def _sb_project(j, x, g_pre, w_in):
    t, d = x.shape
    tm = TOKEN_TILE
    row = lambda i: (i, 0)
    out = jax.ShapeDtypeStruct((t, d), BF16)
    return pl.pallas_call(
        _sb_proj_body,
        grid=(t // tm,),
        in_specs=[
            pl.BlockSpec((tm, d), row),
            pl.BlockSpec((1, d), lambda i: (0, 0)),
            _const_spec((1, d, 3 * d), lambda i: (j, 0, 0)),
        ],
        out_specs=[pl.BlockSpec((tm, d), row)] * 3,
        out_shape=[out, out, out],
        compiler_params=_params("arbitrary"),
        name="sb_project",
    )(x, g_pre, w_in)


def _sb_attn_body(q_ref, k_ref, v_ref, cum_ref, o_ref):
    tq = q_ref.shape[1]
    n_pairs = q_ref.shape[2] // LANES
    i = pl.program_id(2)
    first = lax.broadcasted_iota(jnp.int32, (tq, LANES), 1) < SB_HEAD_DIM
    row = lax.broadcasted_iota(jnp.int32, (2 * tq, SB_BLOCK), 0) & (tq - 1)
    col = lax.broadcasted_iota(jnp.int32, (2 * tq, SB_BLOCK), 1)
    diagonal = col < row
    cum = cum_ref[...]

    def stacked_queries(p):
        q = q_ref[0, :, p * LANES:(p + 1) * LANES]
        zero = jnp.zeros_like(q)
        return jnp.concatenate([jnp.where(first, q, zero), jnp.where(first, zero, q)], axis=0)

    def rows_of(ref, p, j):
        start = pl.multiple_of(j * SB_BLOCK, SB_BLOCK)
        return ref[0, pl.ds(start, SB_BLOCK), p * LANES:(p + 1) * LANES]

    def walk(tasks, q2s, rs, accs):
        zs = [_dot_nt(q2s[p], rows_of(k_ref, p, j)) for p, j, _, _ in tasks]
        ts = []
        for z, (_, _, mask, _) in zip(zs, tasks):
            drop = jnp.maximum(z, 0.0) + jnp.log(1.0 + jnp.exp(-jnp.abs(z)))
            dm = drop if mask is None else jnp.where(mask, drop, 0.0)
            hi = dm.astype(BF16)
            lo = (dm - hi.astype(F32)).astype(BF16)
            ts.append(_dot(jnp.concatenate([hi, lo], axis=1), cum))
        rs, accs = list(rs), list(accs)
        for z, (p, j, mask, exists), t in zip(zs, tasks, ts):
            a = jnp.exp(z - (t[:, :SB_BLOCK] + rs[p]))
            if mask is not None:
                a = jnp.where(mask, a, 0.0)
            vj = rows_of(v_ref, p, j)
            if exists is not None:
                vj = jnp.where(exists, vj, jnp.zeros_like(vj))
            accs[p] = accs[p] + _dot(a.astype(BF16), vj)
            rs[p] = rs[p] + t[:, SB_BLOCK:]
        return rs, accs

    zeros = jnp.zeros((2 * tq, LANES), F32)
    q2s = [stacked_queries(p) for p in range(n_pairs)]
    tasks = []
    for p in range(n_pairs):
        tasks.append((p, i, diagonal, None))
        for u in range(1, SB_STATIC_BLOCKS):
            tasks.append((p, jnp.maximum(i - u, 0), None, i - u >= 0))
    rs, accs = walk(tasks, q2s, [zeros] * n_pairs, [zeros] * n_pairs)

    def more_to_come(carry):
        j, rs, _ = carry
        return jnp.logical_and(j >= 0, jnp.min(functools.reduce(jnp.minimum, rs)) < -SB_LOG_ZERO)

    def one_more_block(carry):
        j, rs, accs = carry
        rs, accs = walk([(p, j, None, None) for p in range(n_pairs)], q2s, rs, accs)
        return j - 1, rs, accs

    _, _, accs = lax.while_loop(more_to_come, one_more_block, (i - SB_STATIC_BLOCKS, rs, accs))
    for p, acc in enumerate(accs):
        o_ref[0, :, p * LANES:(p + 1) * LANES] = jnp.where(first, acc[:tq], acc[tq:]).astype(BF16)


def _sb_cumsum_matrix():
    n = SB_BLOCK
    tri = (np.arange(n)[:, None] >= np.arange(n)[None, :]).astype(np.float32)
    half = np.concatenate([tri, np.ones((n, n), np.float32)], axis=1)
    return jnp.asarray(np.concatenate([half, half], axis=0), dtype=BF16)


def _sb_attention(q, k, v):
    b, s, d = q.shape
    tq = SB_BLOCK
    width = SB_PAIRS_PER_STEP * LANES
    blk = lambda bi, hp, i: (bi, i, hp)
    full = lambda bi, hp, i: (bi, 0, hp)
    return pl.pallas_call(
        _sb_attn_body,
        grid=(b, d // width, s // tq),
        in_specs=[
            pl.BlockSpec((1, tq, width), blk),
            pl.BlockSpec((1, s, width), full),
            pl.BlockSpec((1, s, width), full),
            pl.BlockSpec((2 * SB_BLOCK, 2 * SB_BLOCK), lambda bi, hp, i: (0, 0)),
        ],
        out_specs=pl.BlockSpec((1, tq, width), blk),
        out_shape=jax.ShapeDtypeStruct((b, s, d), BF16),
        compiler_params=_params("arbitrary", "arbitrary", "arbitrary"),
        name="sb_attention",
    )(q, k, v, _sb_cumsum_matrix())


def _ret_proj_body(x_ref, g_ref, w_ref, wkt_ref, cos_ref, sin_ref, cost_ref, sint_ref,
                   q_ref, kt_ref, v_ref, gate_ref):
    d = x_ref.shape[-1]
    dk = d // RET_HEADS
    half = dk // 2
    hn = _rms(x_ref[...], g_ref[...]).astype(BF16)
    cos, sin = cos_ref[...], sin_ref[...]
    cos_t, sin_t = cost_ref[...], sint_ref[...]
    for h in range(RET_HEADS):
        a = slice(h * dk, h * dk + half)
        bsl = slice(h * dk + half, (h + 1) * dk)
        x1 = _dot(hn, w_ref[0, :, a])
        x2 = _dot(hn, w_ref[0, :, bsl])
        q_ref[:, a] = (x1 * cos - x2 * sin).astype(BF16)
        q_ref[:, bsl] = (x1 * sin + x2 * cos).astype(BF16)
        y1 = _dot_nt(wkt_ref[0, a, :], hn)
        y2 = _dot_nt(wkt_ref[0, bsl, :], hn)
        kt_ref[0, a, :] = ((y1 * cos_t - y2 * sin_t) * (dk ** -0.5)).astype(BF16)
        kt_ref[0, bsl, :] = ((y1 * sin_t + y2 * cos_t) * (dk ** -0.5)).astype(BF16)
    for c0 in range(0, 2 * d, 4 * LANES):
        v_ref[:, c0:c0 + 4 * LANES] = _dot(hn, w_ref[0, :, 2 * d + c0:2 * d + c0 + 4 * LANES]).astype(BF16)
        gate_ref[:, c0:c0 + 4 * LANES] = _dot(hn, w_ref[0, :, 4 * d + c0:4 * d + c0 + 4 * LANES])


def _ret_project(j, x, g_pre, w_in, w_kt, seq):
    t, d = x.shape
    tm = TOKEN_TILE
    b = t // seq
    tiles_per_seq = seq // tm
    half = d // RET_HEADS // 2
    pos = jnp.arange(seq, dtype=F32)
    inv_freq = ROPE_BASE ** (-jnp.arange(half, dtype=F32) / half)
    ang = pos[:, None] * inv_freq[None, :]
    cos, sin = jnp.cos(ang), jnp.sin(ang)
    row = lambda i: (i, 0)
    tab = pl.BlockSpec((tm, half), lambda i: (i % tiles_per_seq, 0))
    tab_t = pl.BlockSpec((half, tm), lambda i: (0, i % tiles_per_seq))
    return pl.pallas_call(
        _ret_proj_body,
        grid=(t // tm,),
        in_specs=[
            pl.BlockSpec((tm, d), row),
            pl.BlockSpec((1, d), lambda i: (0, 0)),
            _const_spec((1, d, 6 * d), lambda i: (j, 0, 0)),
            _const_spec((1, d, d), lambda i: (j, 0, 0)),
            tab, tab, tab_t, tab_t,
        ],
        out_specs=[
            pl.BlockSpec((tm, d), row),
            pl.BlockSpec((1, d, tm), lambda i: (i // tiles_per_seq, 0, i % tiles_per_seq)),
            pl.BlockSpec((tm, 2 * d), row),
            pl.BlockSpec((tm, 2 * d), row),
        ],
        out_shape=[
            jax.ShapeDtypeStruct((t, d), BF16),
            jax.ShapeDtypeStruct((b, d, seq), BF16),
            jax.ShapeDtypeStruct((t, 2 * d), BF16),
            jax.ShapeDtypeStruct((t, 2 * d), F32),
        ],
        compiler_params=_params("arbitrary"),
        name="ret_project",
    )(x, g_pre, w_in, w_kt, cos, sin, cos.T, sin.T)


def _ret_body(q_ref, kt_ref, v_ref, gate_ref, gn_ref, intra_ref, qdec_ref, kdec_ref, cdec_ref,
              o_ref, state):
    @pl.when(pl.program_id(2) == 0)
    def _():
        state[...] = jnp.zeros_like(state)

    per_chunk = RET_CHUNK // RET_REF_CHUNK
    for sub in range(RET_STEP // RET_CHUNK):
        rows = slice(sub * RET_CHUNK, (sub + 1) * RET_CHUNK)
        q = q_ref[0, rows, :]
        kt = kt_ref[0, :, rows]
        v = v_ref[0, rows, :]
        scores = (_dot(q, kt) * intra_ref[0]).astype(BF16)
        o = _dot(scores, v) + _dot(q, state[...].astype(BF16)) * qdec_ref[0]
        kt_dec = (kt.astype(F32) * kdec_ref[0]).astype(BF16)
        state[...] = state[...] * cdec_ref[0] + _dot(kt_dec, v)

        mu = jnp.mean(o, axis=-1, keepdims=True)
        oc = o - mu
        var = jnp.mean(oc * oc, axis=-1, keepdims=True)
        on = oc * lax.rsqrt(var + GN_EPS) * gn_ref[...]
        for cl in range(per_chunk):
            c = sub * per_chunk + cl
            g = gate_ref[0, :, c, :]
            o_ref[0, :, c, :] = g * _sigmoid(g) * on[cl * RET_REF_CHUNK:(cl + 1) * RET_REF_CHUNK, :]


def _ret_decay_tables(chunk):
    h = jnp.arange(RET_HEADS, dtype=F32)
    log_gamma = jnp.log1p(-jnp.exp2(-5.0 - h))
    n = jnp.arange(chunk, dtype=F32)
    diff = n[:, None] - n[None, :]
    intra = jnp.where(diff >= 0, jnp.exp(jnp.maximum(diff, 0.0) * log_gamma[:, None, None]), 0.0)
    qdec = jnp.exp((n + 1.0)[None, :] * log_gamma[:, None])[..., None]
    kdec = jnp.exp((chunk - 1.0 - n)[None, :] * log_gamma[:, None])[:, None]
    cdec = jnp.exp(chunk * log_gamma)[:, None, None]
    return intra, qdec, kdec, cdec


def _retention(q, kt, v, gate, gn):
    b, s, d = q.shape
    c = RET_CHUNK
    step = RET_STEP
    dk = d // RET_HEADS
    dv = 2 * d // RET_HEADS
    n_ref_chunks = s // RET_REF_CHUNK
    ref_chunks_per_step = step // RET_REF_CHUNK
    intra, qdec, kdec, cdec = _ret_decay_tables(c)
    head = lambda bi, h, ci: (h, 0, 0)
    permuted = pl.BlockSpec((1, RET_REF_CHUNK, ref_chunks_per_step, dv), lambda bi, h, ci: (bi, 0, ci, h))
    out = pl.pallas_call(
        _ret_body,
        grid=(b, RET_HEADS, s // step),
        in_specs=[
            pl.BlockSpec((1, step, dk), lambda bi, h, ci: (bi, ci, h)),
            pl.BlockSpec((1, dk, step), lambda bi, h, ci: (bi, h, ci)),
            pl.BlockSpec((1, step, dv), lambda bi, h, ci: (bi, ci, h)),
            permuted,
            pl.BlockSpec((1, dv), lambda bi, h, ci: (0, h)),
            pl.BlockSpec((1, c, c), head),
            pl.BlockSpec((1, c, 1), head),
            pl.BlockSpec((1, 1, c), head),
            pl.BlockSpec((1, 1, 1), head),
        ],
        out_specs=permuted,
        out_shape=jax.ShapeDtypeStruct((b, RET_REF_CHUNK, n_ref_chunks, 2 * d), F32),
        scratch_shapes=[pltpu.VMEM((dk, dv), F32)],
        compiler_params=_params("arbitrary", "arbitrary", "arbitrary"),
        name="retention",
    )(q, kt, v, gate.reshape(b, RET_REF_CHUNK, n_ref_chunks, 2 * d), gn, intra, qdec, kdec, cdec)
    return out.reshape(b * s, 2 * d)


def kernel(x, mem, norm_mix_pre, norm_mix_post, norm_xa_pre, norm_xa_post, norm_mem, norm_ffn_pre, norm_ffn_post, pool_w, pool_scale, sb_w_in, sb_w_out, ret_w_in, ret_gn, ret_w_out, xa_w_q, xa_w_kv, xa_w_o, ffn_w_in, ffn_w_out):
    b, s, d = x.shape
    depth = norm_mix_pre.shape[0]
    t = b * s
    assert s % TOKEN_TILE == 0 and s % RET_STEP == 0 and s % SB_BLOCK == 0

    bf = lambda w: w.astype(BF16)
    pool_w, sb_w_in, sb_w_out, ret_w_in, ret_w_out = map(bf, (pool_w, sb_w_in, sb_w_out, ret_w_in, ret_w_out))
    xa_w_q, xa_w_kv, xa_w_o, ffn_w_in, ffn_w_out = map(bf, (xa_w_q, xa_w_kv, xa_w_o, ffn_w_in, ffn_w_out))
    ret_w_kt = jnp.swapaxes(ret_w_in[:, :, d:2 * d], 1, 2)
    gain3 = lambda g: g.reshape(depth, 1, d)
    post_gains = tuple(map(gain3, (norm_mix_post, norm_xa_pre, norm_xa_post, norm_ffn_pre, norm_ffn_post)))

    kmem, vmem = _memory_kv(mem, norm_mem, xa_w_kv)

    xf = x.reshape(t, d)
    for i in range(depth):
        kind, j = i % N_MIXERS, i // N_MIXERS
        g_pre = norm_mix_pre[i].reshape(1, d)
        w_mix = None
        if kind == 0:
            m = _pool_mixer(j, xf.reshape(b, s, d), g_pre, pool_w, pool_scale.reshape(-1, 1, d)).reshape(t, d)
        elif kind == 1:
            q, k, v = _sb_project(j, xf, g_pre, sb_w_in)
            shape3 = lambda a: a.reshape(b, s, d)
            m = _sb_attention(shape3(q), shape3(k), shape3(v)).reshape(t, d)
            w_mix = (j, sb_w_out)
        else:
            q, kt, v, gate = _ret_project(j, xf, g_pre, ret_w_in, ret_w_kt, s)
            m = _retention(q.reshape(b, s, d), kt, v.reshape(b, s, 2 * d), gate, ret_gn[j].reshape(1, 2 * d))
            w_mix = (j, ret_w_out)
        xf = _post_layer(i, xf, m, w_mix, post_gains, kmem, vmem, xa_w_q, xa_w_o, ffn_w_in, ffn_w_out, s)
    return xf.reshape(b, s, d)
```

```python
import functools
import math

import jax
import jax.numpy as jnp
import numpy as np
from jax import lax
from jax.experimental import pallas as pl
from jax.experimental.pallas import tpu as pltpu

N_MIXERS = 3
RMS_EPS = 1e-6
GN_EPS = 1e-5
POOL_WINDOWS = (2, 4, 8, 16)
POOL_HALO = 16
POOL_PAD = 8
SB_HEADS = 16
SB_HEAD_DIM = 64
RET_HEADS = 4
ROPE_BASE = 10000.0
XA_HEADS = 4

LANES = 128
VMEM_LIMIT_BYTES = 56 * 1024 * 1024

TOKEN_TILE = 256
POST_TILE = 512
SB_BLOCK = 128
RET_REF_CHUNK = 64
RET_CHUNK = 256
RET_STEP = 512
SB_PAIRS_PER_STEP = 4
SB_STATIC_BLOCKS = 3
SB_LOG_ZERO = -87.5

BF16 = jnp.bfloat16
F32 = jnp.float32


def _params(*sem):
    return pltpu.CompilerParams(dimension_semantics=sem, vmem_limit_bytes=VMEM_LIMIT_BYTES)


def _const_spec(shape, index_map):
    return pl.BlockSpec(shape, index_map, pipeline_mode=pl.Buffered(1))


def _rms(x, g):
    ms = jnp.mean(x * x, axis=-1, keepdims=True)
    return x * lax.rsqrt(ms + RMS_EPS) * g


def _dot(a, b):
    return jnp.dot(a, b, preferred_element_type=F32)


def _dot_nt(a, b):
    return lax.dot_general(a, b, (((1,), (1,)), ((), ())), preferred_element_type=F32)


def _silu(x):
    h = 0.5 * x
    return h + h * jnp.tanh(h)


def _kv_body(mem_ref, g_ref, w_ref, k_ref, v_ref):
    d = mem_ref.shape[-1]
    mn = _rms(mem_ref[0], g_ref[0]).astype(BF16)
    k_ref[0, 0] = _dot(mn, w_ref[0, :, :d]).astype(BF16)
    v_ref[0, 0] = _dot(mn, w_ref[0, :, d:]).astype(BF16)


def _memory_kv(mem, norm_mem, w_kv):
    b, m, d = mem.shape
    depth = w_kv.shape[0]
    out = jax.ShapeDtypeStruct((depth, b, m, d), BF16)
    return pl.pallas_call(
        _kv_body,
        grid=(depth, b),
        in_specs=[
            pl.BlockSpec((1, m, d), lambda i, j: (j, 0, 0)),
            pl.BlockSpec((1, 1, d), lambda i, j: (i, 0, 0)),
            pl.BlockSpec((1, d, 2 * d), lambda i, j: (i, 0, 0)),
        ],
        out_specs=[pl.BlockSpec((1, 1, m, d), lambda i, j: (i, j, 0, 0))] * 2,
        out_shape=[out, out],
        compiler_params=_params("arbitrary", "arbitrary"),
        name="memory_kv",
    )(mem, norm_mem.reshape(depth, 1, d), w_kv)


def _post_body(has_wmix, x_ref, m_ref, *refs):
    if has_wmix:
        wmix_ref, refs = refs[0], refs[1:]
    (g_mix, g_xa_pre, g_xa_post, g_ffn_pre, g_ffn_post, k_ref, v_ref,
     wq_ref, wo_ref, win_ref, wout_ref, o_ref) = refs
    d = x_ref.shape[-1]
    x = x_ref[...]

    m = _dot(m_ref[...].astype(BF16), wmix_ref[0]) if has_wmix else m_ref[...]
    x = x + _rms(m, g_mix[0])

    hd = d // XA_HEADS
    h = _rms(x, g_xa_pre[0]).astype(BF16)
    q = _dot(h, wq_ref[0]).astype(BF16)
    heads = [slice(hh * hd, (hh + 1) * hd) for hh in range(XA_HEADS)]
    scores = [_dot_nt(q[:, sl], k_ref[0, 0, :, sl]) * (hd ** -0.5) for sl in heads]
    probs = []
    for s in scores:
        e = jnp.exp(s - jnp.max(s, axis=-1, keepdims=True))
        probs.append((e * (1.0 / jnp.sum(e, axis=-1, keepdims=True))).astype(BF16))
    outs = [_dot(p, v_ref[0, 0, :, sl]).astype(BF16) for p, sl in zip(probs, heads)]
    c = functools.reduce(lambda a, b: a + b, [_dot(oh, wo_ref[0, sl, :]) for oh, sl in zip(outs, heads)])
    x = x + _rms(c, g_xa_post[0])

    dff = wout_ref.shape[1]
    fc = 2 * LANES
    h = _rms(x, g_ffn_pre[0]).astype(BF16)
    f = None
    for c0 in range(0, dff, fc):
        gate = _dot(h, win_ref[0, :, c0:c0 + fc])
        up = _dot(h, win_ref[0, :, dff + c0:dff + c0 + fc])
        act = (_silu(gate) * up).astype(BF16)
        fch = _dot(act, wout_ref[0, c0:c0 + fc, :])
        f = fch if f is None else f + fch
    o_ref[...] = x + _rms(f, g_ffn_post[0])


def _post_layer(layer, x, m, w_mix, gains, kmem, vmem, w_q, w_o, w_in, w_out, seq):
    t, d = x.shape
    tm = POST_TILE
    tiles_per_seq = seq // tm
    dff = w_out.shape[1]
    mem_len = kmem.shape[2]
    row = lambda i: (i, 0)
    lay = lambda i: (layer, 0, 0)
    in_specs = [pl.BlockSpec((tm, d), row), pl.BlockSpec((tm, m.shape[1]), row)]
    args = [x, m]
    if w_mix is not None:
        j, w_stack = w_mix
        in_specs.append(_const_spec((1,) + w_stack.shape[1:], lambda i: (j, 0, 0)))
        args.append(w_stack)
    for g in gains:
        in_specs.append(pl.BlockSpec((1, 1, d), lay))
        args.append(g)
    kv_spec = pl.BlockSpec((1, 1, mem_len, d), lambda i: (layer, i // tiles_per_seq, 0, 0))
    in_specs += [
        kv_spec, kv_spec,
        _const_spec((1, d, d), lay), _const_spec((1, d, d), lay),
        _const_spec((1, d, 2 * dff), lay), _const_spec((1, dff, d), lay),
    ]
    args += [kmem, vmem, w_q, w_o, w_in, w_out]
    return pl.pallas_call(
        functools.partial(_post_body, w_mix is not None),
        grid=(t // tm,),
        in_specs=in_specs,
        out_specs=pl.BlockSpec((tm, d), row),
        out_shape=jax.ShapeDtypeStruct((t, d), F32),
        compiler_params=_params("arbitrary"),
        name="post_mixer",
    )(*args)


def _pool_body(x_ref, g_ref, w_ref, scale_ref, o_ref, hbuf, sbuf):
    tm, d = x_ref.shape[1], x_ref.shape[2]
    gd = d // len(POOL_WINDOWS)
    s_idx = pl.program_id(1)
    lo, mid, hi = POOL_PAD, POOL_PAD + POOL_HALO, POOL_PAD + POOL_HALO + tm

    @pl.when(s_idx == 0)
    def _():
        hbuf[0:mid, :] = jnp.zeros((mid, d), F32)
        sbuf[0:lo, :] = jnp.zeros((lo, gd), F32)

    @pl.when(s_idx != 0)
    def _():
        hbuf[lo:mid, :] = hbuf[hi - POOL_HALO:hi, :]

    hn = _rms(x_ref[0], g_ref[...])
    hbuf[mid:hi, :] = hn

    t = s_idx * tm + lax.broadcasted_iota(jnp.int32, (tm, 1), 0)
    for g, w in enumerate(POOL_WINDOWS):
        cols = slice(g * gd, (g + 1) * gd)
        acc = hbuf[lo:hi, cols] + hbuf[lo - 1:hi - 1, cols]
        shift = 2
        while shift < w:
            sbuf[lo:hi, :] = acc
            acc = acc + sbuf[lo - shift:hi - shift, :]
            shift *= 2
        inv = 1.0 / jnp.minimum(t + 1, w).astype(F32)
        dlt = (acc[POOL_HALO:] * inv - hn[:, cols]).astype(BF16)
        o_ref[0, :, cols] = _dot(dlt, w_ref[0, g]) * scale_ref[0, :, cols]


def _pool_mixer(j, x3, g_pre, pool_w, pool_scale):
    b, s, d = x3.shape
    tm = TOKEN_TILE
    ng, gd = pool_w.shape[1], pool_w.shape[2]
    return pl.pallas_call(
        _pool_body,
        grid=(b, s // tm),
        in_specs=[
            pl.BlockSpec((1, tm, d), lambda i, k: (i, k, 0)),
            pl.BlockSpec((1, d), lambda i, k: (0, 0)),
            pl.BlockSpec((1, ng, gd, gd), lambda i, k: (j, 0, 0, 0)),
            pl.BlockSpec((1, 1, d), lambda i, k: (j, 0, 0)),
        ],
        out_specs=pl.BlockSpec((1, tm, d), lambda i, k: (i, k, 0)),
        out_shape=jax.ShapeDtypeStruct((b, s, d), F32),
        scratch_shapes=[pltpu.VMEM((POOL_PAD + POOL_HALO + tm, d), F32),
                        pltpu.VMEM((POOL_PAD + POOL_HALO + tm, gd), F32)],
        compiler_params=_params("arbitrary", "arbitrary"),
        name="pool_mixer",
    )(x3, g_pre, pool_w, pool_scale)


def _sb_proj_body(x_ref, g_ref, w_ref, q_ref, k_ref, v_ref):
    d = x_ref.shape[-1]
    hn = _rms(x_ref[...], g_ref[...]).astype(BF16)
    q_ref[...] = (_dot(hn, w_ref[0, :, :d]) * (SB_HEAD_DIM ** -0.5)).astype(BF16)
    k_ref[...] = _dot(hn, w_ref[0, :, d:2 * d]).astype(BF16)
    v_ref[...] = _dot(hn, w_ref[0, :, 2 * d:]).astype(BF16)


def _sb_project(j, x, g_pre, w_in):
    t, d = x.shape
    tm = TOKEN_TILE
    row = lambda i: (i, 0)
    out = jax.ShapeDtypeStruct((t, d), BF16)
    return pl.pallas_call(
        _sb_proj_body,
        grid=(t // tm,),
        in_specs=[
            pl.BlockSpec((tm, d), row),
            pl.BlockSpec((1, d), lambda i: (0, 0)),
            _const_spec((1, d, 3 * d), lambda i: (j, 0, 0)),
        ],
        out_specs=[pl.BlockSpec((tm, d), row)] * 3,
        out_shape=[out, out, out],
        compiler_params=_params("arbitrary"),
        name="sb_project",
    )(x, g_pre, w_in)


def _sb_attn_body(q_ref, k_ref, v_ref, cum_ref, o_ref):
    tq = q_ref.shape[1]
    n_pairs = q_ref.shape[2] // LANES
    i = pl.program_id(2)
    first = lax.broadcasted_iota(jnp.int32, (tq, LANES), 1) < SB_HEAD_DIM
    row = lax.broadcasted_iota(jnp.int32, (2 * tq, SB_BLOCK), 0) & (tq - 1)
    col = lax.broadcasted_iota(jnp.int32, (2 * tq, SB_BLOCK), 1)
    diagonal = col < row
    cum = cum_ref[...]

    def stacked_queries(p):
        q = q_ref[0, :, p * LANES:(p + 1) * LANES]
        zero = jnp.zeros_like(q)
        return jnp.concatenate([jnp.where(first, q, zero), jnp.where(first, zero, q)], axis=0)

    def rows_of(ref, p, j):
        start = pl.multiple_of(j * SB_BLOCK, SB_BLOCK)
        return ref[0, pl.ds(start, SB_BLOCK), p * LANES:(p + 1) * LANES]

    def walk(tasks, q2s, rs, accs):
        zs = [_dot_nt(q2s[p], rows_of(k_ref, p, j)) for p, j, _, _ in tasks]
        ts = []
        for z, (_, _, mask, _) in zip(zs, tasks):
            drop = jnp.maximum(z, 0.0) + jnp.log(1.0 + jnp.exp(-jnp.abs(z)))
            dm = drop if mask is None else jnp.where(mask, drop, 0.0)
            hi = dm.astype(BF16)
            lo = (dm - hi.astype(F32)).astype(BF16)
            ts.append(_dot(jnp.concatenate([hi, lo], axis=1), cum))
        rs, accs = list(rs), list(accs)
        for z, (p, j, mask, exists), t in zip(zs, tasks, ts):
            a = jnp.exp(z - (t[:, :SB_BLOCK] + rs[p]))
            if mask is not None:
                a = jnp.where(mask, a, 0.0)
            vj = rows_of(v_ref, p, j)
            if exists is not None:
                vj = jnp.where(exists, vj, jnp.zeros_like(vj))
            accs[p] = accs[p] + _dot(a.astype(BF16), vj)
            rs[p] = rs[p] + t[:, SB_BLOCK:]
        return rs, accs

    zeros = jnp.zeros((2 * tq, LANES), F32)
    q2s = [stacked_queries(p) for p in range(n_pairs)]
    tasks = []
    for p in range(n_pairs):
        tasks.append((p, i, diagonal, None))
        for u in range(1, SB_STATIC_BLOCKS):
            tasks.append((p, jnp.maximum(i - u, 0), None, i - u >= 0))
    rs, accs = walk(tasks, q2s, [zeros] * n_pairs, [zeros] * n_pairs)

    def more_to_come(carry):
        j, rs, _ = carry
        return jnp.logical_and(j >= 0, jnp.min(functools.reduce(jnp.minimum, rs)) < -SB_LOG_ZERO)

    def one_more_block(carry):
        j, rs, accs = carry
        rs, accs = walk([(p, j, None, None) for p in range(n_pairs)], q2s, rs, accs)
        return j - 1, rs, accs

    _, _, accs = lax.while_loop(more_to_come, one_more_block, (i - SB_STATIC_BLOCKS, rs, accs))
    for p, acc in enumerate(accs):
        o_ref[0, :, p * LANES:(p + 1) * LANES] = jnp.where(first, acc[:tq], acc[tq:]).astype(BF16)


def _sb_cumsum_matrix():
    n = SB_BLOCK
    tri = (np.arange(n)[:, None] >= np.arange(n)[None, :]).astype(np.float32)
    half = np.concatenate([tri, np.ones((n, n), np.float32)], axis=1)
    return jnp.asarray(np.concatenate([half, half], axis=0), dtype=BF16)


def _sb_attention(q, k, v):
    b, s, d = q.shape
    tq = SB_BLOCK
    width = SB_PAIRS_PER_STEP * LANES
    blk = lambda bi, hp, i: (bi, i, hp)
    full = lambda bi, hp, i: (bi, 0, hp)
    return pl.pallas_call(
        _sb_attn_body,
        grid=(b, d // width, s // tq),
        in_specs=[
            pl.BlockSpec((1, tq, width), blk),
            pl.BlockSpec((1, s, width), full),
            pl.BlockSpec((1, s, width), full),
            pl.BlockSpec((2 * SB_BLOCK, 2 * SB_BLOCK), lambda bi, hp, i: (0, 0)),
        ],
        out_specs=pl.BlockSpec((1, tq, width), blk),
        out_shape=jax.ShapeDtypeStruct((b, s, d), BF16),
        compiler_params=_params("arbitrary", "arbitrary", "arbitrary"),
        name="sb_attention",
    )(q, k, v, _sb_cumsum_matrix())


def _ret_proj_body(x_ref, g_ref, w_ref, wkt_ref, cos_ref, sin_ref, cost_ref, sint_ref,
                   q_ref, kt_ref, v_ref, gate_ref):
    d = x_ref.shape[-1]
    dk = d // RET_HEADS
    half = dk // 2
    hn = _rms(x_ref[...], g_ref[...]).astype(BF16)
    cos, sin = cos_ref[...], sin_ref[...]
    cos_t, sin_t = cost_ref[...], sint_ref[...]
    for h in range(RET_HEADS):
        a = slice(h * dk, h * dk + half)
        bsl = slice(h * dk + half, (h + 1) * dk)
        xh = _dot(hn, w_ref[0, :, h * dk:(h + 1) * dk])
        x1, x2 = xh[:, :half], xh[:, half:]
        q_ref[:, a] = (x1 * cos - x2 * sin).astype(BF16)
        q_ref[:, bsl] = (x1 * sin + x2 * cos).astype(BF16)
        yh = _dot_nt(wkt_ref[0, h * dk:(h + 1) * dk, :], hn)
        y1, y2 = yh[:half], yh[half:]
        kt_ref[0, a, :] = ((y1 * cos_t - y2 * sin_t) * (dk ** -0.5)).astype(BF16)
        kt_ref[0, bsl, :] = ((y1 * sin_t + y2 * cos_t) * (dk ** -0.5)).astype(BF16)
    for c0 in range(0, 2 * d, 4 * LANES):
        v_ref[:, c0:c0 + 4 * LANES] = _dot(hn, w_ref[0, :, 2 * d + c0:2 * d + c0 + 4 * LANES]).astype(BF16)
        gate_ref[:, c0:c0 + 4 * LANES] = _dot(hn, w_ref[0, :, 4 * d + c0:4 * d + c0 + 4 * LANES])


def _ret_project(j, x, g_pre, w_in, w_kt, seq):
    t, d = x.shape
    tm = TOKEN_TILE
    b = t // seq
    tiles_per_seq = seq // tm
    half = d // RET_HEADS // 2
    pos = jnp.arange(seq, dtype=F32)
    inv_freq = ROPE_BASE ** (-jnp.arange(half, dtype=F32) / half)
    ang = pos[:, None] * inv_freq[None, :]
    cos, sin = jnp.cos(ang), jnp.sin(ang)
    row = lambda i: (i, 0)
    tab = pl.BlockSpec((tm, half), lambda i: (i % tiles_per_seq, 0))
    tab_t = pl.BlockSpec((half, tm), lambda i: (0, i % tiles_per_seq))
    return pl.pallas_call(
        _ret_proj_body,
        grid=(t // tm,),
        in_specs=[
            pl.BlockSpec((tm, d), row),
            pl.BlockSpec((1, d), lambda i: (0, 0)),
            _const_spec((1, d, 6 * d), lambda i: (j, 0, 0)),
            _const_spec((1, d, d), lambda i: (j, 0, 0)),
            tab, tab, tab_t, tab_t,
        ],
        out_specs=[
            pl.BlockSpec((tm, d), row),
            pl.BlockSpec((1, d, tm), lambda i: (i // tiles_per_seq, 0, i % tiles_per_seq)),
            pl.BlockSpec((tm, 2 * d), row),
            pl.BlockSpec((tm, 2 * d), row),
        ],
        out_shape=[
            jax.ShapeDtypeStruct((t, d), BF16),
            jax.ShapeDtypeStruct((b, d, seq), BF16),
            jax.ShapeDtypeStruct((t, 2 * d), BF16),
            jax.ShapeDtypeStruct((t, 2 * d), F32),
        ],
        compiler_params=_params("arbitrary"),
        name="ret_project",
    )(x, g_pre, w_in, w_kt, cos, sin, cos.T, sin.T)


def _ret_body(q_ref, kt_ref, v_ref, gate_ref, gn_ref, intra_ref, qdec_ref, kdec_ref, cdec_ref,
              o_ref, state):
    @pl.when(pl.program_id(2) == 0)
    def _():
        state[...] = jnp.zeros_like(state)

    per_chunk = RET_CHUNK // RET_REF_CHUNK
    for sub in range(RET_STEP // RET_CHUNK):
        rows = slice(sub * RET_CHUNK, (sub + 1) * RET_CHUNK)
        q = q_ref[0, rows, :]
        kt = kt_ref[0, :, rows]
        v = v_ref[0, rows, :]
        scores = (_dot(q, kt) * intra_ref[0]).astype(BF16)
        o = _dot(scores, v) + _dot(q, state[...].astype(BF16)) * qdec_ref[0]
        kt_dec = (kt.astype(F32) * kdec_ref[0]).astype(BF16)
        state[...] = state[...] * cdec_ref[0] + _dot(kt_dec, v)

        mu = jnp.mean(o, axis=-1, keepdims=True)
        oc = o - mu
        var = jnp.mean(oc * oc, axis=-1, keepdims=True)
        on = oc * lax.rsqrt(var + GN_EPS) * gn_ref[...]
        for cl in range(per_chunk):
            c = sub * per_chunk + cl
            g = gate_ref[0, :, c, :]
            o_ref[0, :, c, :] = _silu(g) * on[cl * RET_REF_CHUNK:(cl + 1) * RET_REF_CHUNK, :]


def _ret_decay_tables(chunk):
    h = jnp.arange(RET_HEADS, dtype=F32)
    log_gamma = jnp.log1p(-jnp.exp2(-5.0 - h))
    n = jnp.arange(chunk, dtype=F32)
    diff = n[:, None] - n[None, :]
    intra = jnp.where(diff >= 0, jnp.exp(jnp.maximum(diff, 0.0) * log_gamma[:, None, None]), 0.0)
    qdec = jnp.exp((n + 1.0)[None, :] * log_gamma[:, None])[..., None]
    kdec = jnp.exp((chunk - 1.0 - n)[None, :] * log_gamma[:, None])[:, None]
    cdec = jnp.exp(chunk * log_gamma)[:, None, None]
    return intra, qdec, kdec, cdec


def _retention(q, kt, v, gate, gn):
    b, s, d = q.shape
    c = RET_CHUNK
    step = RET_STEP
    dk = d // RET_HEADS
    dv = 2 * d // RET_HEADS
    n_ref_chunks = s // RET_REF_CHUNK
    ref_chunks_per_step = step // RET_REF_CHUNK
    intra, qdec, kdec, cdec = _ret_decay_tables(c)
    head = lambda bi, h, ci: (h, 0, 0)
    permuted = pl.BlockSpec((1, RET_REF_CHUNK, ref_chunks_per_step, dv), lambda bi, h, ci: (bi, 0, ci, h))
    out = pl.pallas_call(
        _ret_body,
        grid=(b, RET_HEADS, s // step),
        in_specs=[
            pl.BlockSpec((1, step, dk), lambda bi, h, ci: (bi, ci, h)),
            pl.BlockSpec((1, dk, step), lambda bi, h, ci: (bi, h, ci)),
            pl.BlockSpec((1, step, dv), lambda bi, h, ci: (bi, ci, h)),
            permuted,
            pl.BlockSpec((1, dv), lambda bi, h, ci: (0, h)),
            pl.BlockSpec((1, c, c), head),
            pl.BlockSpec((1, c, 1), head),
            pl.BlockSpec((1, 1, c), head),
            pl.BlockSpec((1, 1, 1), head),
        ],
        out_specs=permuted,
        out_shape=jax.ShapeDtypeStruct((b, RET_REF_CHUNK, n_ref_chunks, 2 * d), F32),
        scratch_shapes=[pltpu.VMEM((dk, dv), F32)],
        compiler_params=_params("arbitrary", "arbitrary", "arbitrary"),
        name="retention",
    )(q, kt, v, gate.reshape(b, RET_REF_CHUNK, n_ref_chunks, 2 * d), gn, intra, qdec, kdec, cdec)
    return out.reshape(b * s, 2 * d)


def kernel(x, mem, norm_mix_pre, norm_mix_post, norm_xa_pre, norm_xa_post, norm_mem, norm_ffn_pre, norm_ffn_post, pool_w, pool_scale, sb_w_in, sb_w_out, ret_w_in, ret_gn, ret_w_out, xa_w_q, xa_w_kv, xa_w_o, ffn_w_in, ffn_w_out):
    b, s, d = x.shape
    depth = norm_mix_pre.shape[0]
    t = b * s
    assert s % TOKEN_TILE == 0 and s % RET_STEP == 0 and s % SB_BLOCK == 0

    bf = lambda w: w.astype(BF16)
    pool_w, sb_w_in, sb_w_out, ret_w_in, ret_w_out = map(bf, (pool_w, sb_w_in, sb_w_out, ret_w_in, ret_w_out))
    xa_w_q, xa_w_kv, xa_w_o, ffn_w_in, ffn_w_out = map(bf, (xa_w_q, xa_w_kv, xa_w_o, ffn_w_in, ffn_w_out))
    ret_w_kt = jnp.swapaxes(ret_w_in[:, :, d:2 * d], 1, 2)
    gain3 = lambda g: g.reshape(depth, 1, d)
    post_gains = tuple(map(gain3, (norm_mix_post, norm_xa_pre, norm_xa_post, norm_ffn_pre, norm_ffn_post)))

    kmem, vmem = _memory_kv(mem, norm_mem, xa_w_kv)

    xf = x.reshape(t, d)
    for i in range(depth):
        kind, j = i % N_MIXERS, i // N_MIXERS
        g_pre = norm_mix_pre[i].reshape(1, d)
        w_mix = None
        if kind == 0:
            m = _pool_mixer(j, xf.reshape(b, s, d), g_pre, pool_w, pool_scale.reshape(-1, 1, d)).reshape(t, d)
        elif kind == 1:
            q, k, v = _sb_project(j, xf, g_pre, sb_w_in)
            shape3 = lambda a: a.reshape(b, s, d)
            m = _sb_attention(shape3(q), shape3(k), shape3(v)).reshape(t, d)
            w_mix = (j, sb_w_out)
        else:
            q, kt, v, gate = _ret_project(j, xf, g_pre, ret_w_in, ret_w_kt, s)
            m = _retention(q.reshape(b, s, d), kt, v.reshape(b, s, 2 * d), gate, ret_gn[j].reshape(1, 2 * d))
            w_mix = (j, ret_w_out)
        xf = _post_layer(i, xf, m, w_mix, post_gains, kmem, vmem, xa_w_q, xa_w_o, ffn_w_in, ffn_w_out, s)
    return xf.reshape(b, s, d)
```

```python
import functools
import math

import jax
import jax.numpy as jnp
import numpy as np
from jax import lax
from jax.experimental import pallas as pl
from jax.experimental.pallas import tpu as pltpu

N_MIXERS = 3
RMS_EPS = 1e-6
GN_EPS = 1e-5
POOL_WINDOWS = (2, 4, 8, 16)
POOL_HALO = 16
POOL_PAD = 8
SB_HEADS = 16
SB_HEAD_DIM = 64
RET_HEADS = 4
ROPE_BASE = 10000.0
XA_HEADS = 4

LANES = 128
VMEM_LIMIT_BYTES = 56 * 1024 * 1024

TOKEN_TILE = 256
POST_TILE = 512
POST_PIECE_AFTER_CHUNK = (0, 2, 3, 4, 6, 7, 8)
SB_BLOCK = 128
RET_REF_CHUNK = 64
RET_CHUNK = 256
RET_STEP = 512
SB_PAIRS_PER_STEP = 4
SB_STATIC_BLOCKS = 3
SB_LOG_ZERO = -87.5

BF16 = jnp.bfloat16
F32 = jnp.float32


def _params(*sem):
    return pltpu.CompilerParams(dimension_semantics=sem, vmem_limit_bytes=VMEM_LIMIT_BYTES)


def _const_spec(shape, index_map):
    return pl.BlockSpec(shape, index_map, pipeline_mode=pl.Buffered(1))


def _rms(x, g):
    ms = jnp.mean(x * x, axis=-1, keepdims=True)
    return x * lax.rsqrt(ms + RMS_EPS) * g


def _dot(a, b):
    return jnp.dot(a, b, preferred_element_type=F32)


def _dot_nt(a, b):
    return lax.dot_general(a, b, (((1,), (1,)), ((), ())), preferred_element_type=F32)


def _silu(x):
    h = 0.5 * x
    return h + h * jnp.tanh(h)


def _kv_body(mem_ref, g_ref, w_ref, k_ref, v_ref):
    d = mem_ref.shape[-1]
    mn = _rms(mem_ref[0], g_ref[0]).astype(BF16)
    k_ref[0, 0] = _dot(mn, w_ref[0, :, :d]).astype(BF16)
    v_ref[0, 0] = _dot(mn, w_ref[0, :, d:]).astype(BF16)


def _memory_kv(mem, norm_mem, w_kv):
    b, m, d = mem.shape
    depth = w_kv.shape[0]
    out = jax.ShapeDtypeStruct((depth, b, m, d), BF16)
    return pl.pallas_call(
        _kv_body,
        grid=(depth, b),
        in_specs=[
            pl.BlockSpec((1, m, d), lambda i, j: (j, 0, 0)),
            pl.BlockSpec((1, 1, d), lambda i, j: (i, 0, 0)),
            pl.BlockSpec((1, d, 2 * d), lambda i, j: (i, 0, 0)),
        ],
        out_specs=[pl.BlockSpec((1, 1, m, d), lambda i, j: (i, j, 0, 0))] * 2,
        out_shape=[out, out],
        compiler_params=_params("arbitrary", "arbitrary"),
        name="memory_kv",
    )(mem, norm_mem.reshape(depth, 1, d), w_kv)


def _post_body(has_wmix, x_ref, m_ref, *refs):
    if has_wmix:
        wmix_ref, refs = refs[0], refs[1:]
    (g_mix, g_xa_pre, g_xa_post, g_ffn_pre, g_ffn_post, k_ref, v_ref,
     wq_ref, wo_ref, win_ref, wout_ref, o_ref, x_buf, h_buf) = refs
    d = x_ref.shape[-1]
    hd = d // XA_HEADS
    dff = wout_ref.shape[1]
    fc = 2 * LANES
    step = pl.program_id(0)
    slot_next = step % 2
    slot_now = 1 - slot_next

    @pl.when(step == 0)
    def _():
        x_buf[1] = jnp.zeros(x_buf.shape[1:], x_buf.dtype)
        h_buf[1] = jnp.zeros(h_buf.shape[1:], h_buf.dtype)

    def swiglu_chunks():
        f = None
        for c0 in range(0, dff, fc):
            h = h_buf[slot_now]
            gate = _dot(h, win_ref[0, :, c0:c0 + fc])
            up = _dot(h, win_ref[0, :, dff + c0:dff + c0 + fc])
            act = (_silu(gate) * up).astype(BF16)
            fch = _dot(act, wout_ref[0, c0:c0 + fc, :])
            f = fch if f is None else f + fch
            yield f

    def attention_pieces():
        x = x_ref[...]
        m = _dot(m_ref[...].astype(BF16), wmix_ref[0]) if has_wmix else m_ref[...]
        x = x + _rms(m, g_mix[0])
        h = _rms(x, g_xa_pre[0]).astype(BF16)
        yield
        q = _dot(h, wq_ref[0]).astype(BF16)
        yield
        heads = [slice(hh * hd, (hh + 1) * hd) for hh in range(XA_HEADS)]
        scores = [_dot_nt(q[:, sl], k_ref[0, 0, :, sl]) * (hd ** -0.5) for sl in heads]
        yield
        probs = []
        for s in scores:
            e = jnp.exp(s - jnp.max(s, axis=-1, keepdims=True))
            probs.append((e * (1.0 / jnp.sum(e, axis=-1, keepdims=True))).astype(BF16))
        yield
        outs = [_dot(p, v_ref[0, 0, :, sl]).astype(BF16) for p, sl in zip(probs, heads)]
        yield
        c = functools.reduce(lambda a, b: a + b, [_dot(oh, wo_ref[0, sl, :]) for oh, sl in zip(outs, heads)])
        yield
        x = x + _rms(c, g_xa_post[0])
        x_buf[slot_next] = x
        h_buf[slot_next] = _rms(x, g_ffn_pre[0]).astype(BF16)
        yield

    pieces = attention_pieces()
    f = None
    for n, f in enumerate(swiglu_chunks()):
        if n in POST_PIECE_AFTER_CHUNK:
            next(pieces)
    for _ in pieces:
        pass
    o_ref[...] = x_buf[slot_now] + _rms(f, g_ffn_post[0])


def _post_layer(layer, x, m, w_mix, gains, kmem, vmem, w_q, w_o, w_in, w_out, seq):
    t, d = x.shape
    tm = POST_TILE
    tiles_per_seq = seq // tm
    dff = w_out.shape[1]
    mem_len = kmem.shape[2]
    n_tiles = t // tm
    row = lambda i: (jnp.minimum(i, n_tiles - 1), 0)
    out_row = lambda i: (jnp.maximum(i - 1, 0), 0)
    lay = lambda i: (layer, 0, 0)
    in_specs = [pl.BlockSpec((tm, d), row), pl.BlockSpec((tm, m.shape[1]), row)]
    args = [x, m]
    if w_mix is not None:
        j, w_stack = w_mix
        in_specs.append(_const_spec((1,) + w_stack.shape[1:], lambda i: (j, 0, 0)))
        args.append(w_stack)
    for g in gains:
        in_specs.append(pl.BlockSpec((1, 1, d), lay))
        args.append(g)
    kv_spec = pl.BlockSpec((1, 1, mem_len, d),
                           lambda i: (layer, jnp.minimum(i, n_tiles - 1) // tiles_per_seq, 0, 0))
    in_specs += [
        kv_spec, kv_spec,
        _const_spec((1, d, d), lay), _const_spec((1, d, d), lay),
        _const_spec((1, d, 2 * dff), lay), _const_spec((1, dff, d), lay),
    ]
    args += [kmem, vmem, w_q, w_o, w_in, w_out]
    return pl.pallas_call(
        functools.partial(_post_body, w_mix is not None),
        grid=(n_tiles + 1,),
        in_specs=in_specs,
        out_specs=pl.BlockSpec((tm, d), out_row),
        out_shape=jax.ShapeDtypeStruct((t, d), F32),
        scratch_shapes=[pltpu.VMEM((2, tm, d), F32), pltpu.VMEM((2, tm, d), BF16)],
        compiler_params=_params("arbitrary"),
        name="post_mixer",
    )(*args)


def _pool_body(x_ref, g_ref, w_ref, scale_ref, o_ref, hbuf, sbuf):
    tm, d = x_ref.shape[1], x_ref.shape[2]
    gd = d // len(POOL_WINDOWS)
    s_idx = pl.program_id(1)
    lo, mid, hi = POOL_PAD, POOL_PAD + POOL_HALO, POOL_PAD + POOL_HALO + tm

    @pl.when(s_idx == 0)
    def _():
        hbuf[0:mid, :] = jnp.zeros((mid, d), F32)
        sbuf[0:lo, :] = jnp.zeros((lo, gd), F32)

    @pl.when(s_idx != 0)
    def _():
        hbuf[lo:mid, :] = hbuf[hi - POOL_HALO:hi, :]

    hn = _rms(x_ref[0], g_ref[...])
    hbuf[mid:hi, :] = hn

    t = s_idx * tm + lax.broadcasted_iota(jnp.int32, (tm, 1), 0)
    for g, w in enumerate(POOL_WINDOWS):
        cols = slice(g * gd, (g + 1) * gd)
        acc = hbuf[lo:hi, cols] + hbuf[lo - 1:hi - 1, cols]
        shift = 2
        while shift < w:
            sbuf[lo:hi, :] = acc
            acc = acc + sbuf[lo - shift:hi - shift, :]
            shift *= 2
        inv = 1.0 / jnp.minimum(t + 1, w).astype(F32)
        dlt = (acc[POOL_HALO:] * inv - hn[:, cols]).astype(BF16)
        o_ref[0, :, cols] = _dot(dlt, w_ref[0, g]) * scale_ref[0, :, cols]


def _pool_mixer(j, x3, g_pre, pool_w, pool_scale):
    b, s, d = x3.shape
    tm = TOKEN_TILE
    ng, gd = pool_w.shape[1], pool_w.shape[2]
    return pl.pallas_call(
        _pool_body,
        grid=(b, s // tm),
        in_specs=[
            pl.BlockSpec((1, tm, d), lambda i, k: (i, k, 0)),
            pl.BlockSpec((1, d), lambda i, k: (0, 0)),
            pl.BlockSpec((1, ng, gd, gd), lambda i, k: (j, 0, 0, 0)),
            pl.BlockSpec((1, 1, d), lambda i, k: (j, 0, 0)),
        ],
        out_specs=pl.BlockSpec((1, tm, d), lambda i, k: (i, k, 0)),
        out_shape=jax.ShapeDtypeStruct((b, s, d), F32),
        scratch_shapes=[pltpu.VMEM((POOL_PAD + POOL_HALO + tm, d), F32),
                        pltpu.VMEM((POOL_PAD + POOL_HALO + tm, gd), F32)],
        compiler_params=_params("arbitrary", "arbitrary"),
        name="pool_mixer",
    )(x3, g_pre, pool_w, pool_scale)


def _sb_proj_body(x_ref, g_ref, w_ref, q_ref, k_ref, v_ref):
    d = x_ref.shape[-1]
    hn = _rms(x_ref[...], g_ref[...]).astype(BF16)
    q_ref[...] = (_dot(hn, w_ref[0, :, :d]) * (SB_HEAD_DIM ** -0.5)).astype(BF16)
    k_ref[...] = _dot(hn, w_ref[0, :, d:2 * d]).astype(BF16)
    v_ref[...] = _dot(hn, w_ref[0, :, 2 * d:]).astype(BF16)


def _sb_project(j, x, g_pre, w_in):
    t, d = x.shape
    tm = TOKEN_TILE
    row = lambda i: (i, 0)
    out = jax.ShapeDtypeStruct((t, d), BF16)
    return pl.pallas_call(
        _sb_proj_body,
        grid=(t // tm,),
        in_specs=[
            pl.BlockSpec((tm, d), row),
            pl.BlockSpec((1, d), lambda i: (0, 0)),
            _const_spec((1, d, 3 * d), lambda i: (j, 0, 0)),
        ],
        out_specs=[pl.BlockSpec((tm, d), row)] * 3,
        out_shape=[out, out, out],
        compiler_params=_params("arbitrary"),
        name="sb_project",
    )(x, g_pre, w_in)


def _sb_attn_body(q_ref, k_ref, v_ref, cum_ref, o_ref):
    tq = q_ref.shape[1]
    n_pairs = q_ref.shape[2] // LANES
    i = pl.program_id(2)
    first = lax.broadcasted_iota(jnp.int32, (tq, LANES), 1) < SB_HEAD_DIM
    row = lax.broadcasted_iota(jnp.int32, (2 * tq, SB_BLOCK), 0) & (tq - 1)
    col = lax.broadcasted_iota(jnp.int32, (2 * tq, SB_BLOCK), 1)
    diagonal = col < row
    cum = cum_ref[...]

    def stacked_queries(p):
        q = q_ref[0, :, p * LANES:(p + 1) * LANES]
        zero = jnp.zeros_like(q)
        return jnp.concatenate([jnp.where(first, q, zero), jnp.where(first, zero, q)], axis=0)

    def rows_of(ref, p, j):
        start = pl.multiple_of(j * SB_BLOCK, SB_BLOCK)
        return ref[0, pl.ds(start, SB_BLOCK), p * LANES:(p + 1) * LANES]

    def walk(tasks, q2s, rs, accs):
        zs = [_dot_nt(q2s[p], rows_of(k_ref, p, j)) for p, j, _, _ in tasks]
        ts = []
        for z, (_, _, mask, _) in zip(zs, tasks):
            drop = jnp.maximum(z, 0.0) + jnp.log(1.0 + jnp.exp(-jnp.abs(z)))
            dm = drop if mask is None else jnp.where(mask, drop, 0.0)
            hi = dm.astype(BF16)
            lo = (dm - hi.astype(F32)).astype(BF16)
            ts.append(_dot(jnp.concatenate([hi, lo], axis=1), cum))
        rs, accs = list(rs), list(accs)
        for z, (p, j, mask, exists), t in zip(zs, tasks, ts):
            a = jnp.exp(z - (t[:, :SB_BLOCK] + rs[p]))
            if mask is not None:
                a = jnp.where(mask, a, 0.0)
            vj = rows_of(v_ref, p, j)
            if exists is not None:
                vj = jnp.where(exists, vj, jnp.zeros_like(vj))
            accs[p] = accs[p] + _dot(a.astype(BF16), vj)
            rs[p] = rs[p] + t[:, SB_BLOCK:]
        return rs, accs

    zeros = jnp.zeros((2 * tq, LANES), F32)
    q2s = [stacked_queries(p) for p in range(n_pairs)]
    tasks = []
    for p in range(n_pairs):
        tasks.append((p, i, diagonal, None))
        for u in range(1, SB_STATIC_BLOCKS):
            tasks.append((p, jnp.maximum(i - u, 0), None, i - u >= 0))
    rs, accs = walk(tasks, q2s, [zeros] * n_pairs, [zeros] * n_pairs)

    def more_to_come(carry):
        j, rs, _ = carry
        return jnp.logical_and(j >= 0, jnp.min(functools.reduce(jnp.minimum, rs)) < -SB_LOG_ZERO)

    def one_more_block(carry):
        j, rs, accs = carry
        rs, accs = walk([(p, j, None, None) for p in range(n_pairs)], q2s, rs, accs)
        return j - 1, rs, accs

    _, _, accs = lax.while_loop(more_to_come, one_more_block, (i - SB_STATIC_BLOCKS, rs, accs))
    for p, acc in enumerate(accs):
        o_ref[0, :, p * LANES:(p + 1) * LANES] = jnp.where(first, acc[:tq], acc[tq:]).astype(BF16)


def _sb_cumsum_matrix():
    n = SB_BLOCK
    tri = (np.arange(n)[:, None] >= np.arange(n)[None, :]).astype(np.float32)
    half = np.concatenate([tri, np.ones((n, n), np.float32)], axis=1)
    return jnp.asarray(np.concatenate([half, half], axis=0), dtype=BF16)


def _sb_attention(q, k, v):
    b, s, d = q.shape
    tq = SB_BLOCK
    width = SB_PAIRS_PER_STEP * LANES
    blk = lambda bi, hp, i: (bi, i, hp)
    full = lambda bi, hp, i: (bi, 0, hp)
    return pl.pallas_call(
        _sb_attn_body,
        grid=(b, d // width, s // tq),
        in_specs=[
            pl.BlockSpec((1, tq, width), blk),
            pl.BlockSpec((1, s, width), full),
            pl.BlockSpec((1, s, width), full),
            pl.BlockSpec((2 * SB_BLOCK, 2 * SB_BLOCK), lambda bi, hp, i: (0, 0)),
        ],
        out_specs=pl.BlockSpec((1, tq, width), blk),
        out_shape=jax.ShapeDtypeStruct((b, s, d), BF16),
        compiler_params=_params("arbitrary", "arbitrary", "arbitrary"),
        name="sb_attention",
    )(q, k, v, _sb_cumsum_matrix())


def _ret_proj_body(x_ref, g_ref, w_ref, wkt_ref, cos_ref, sin_ref, cost_ref, sint_ref,
                   q_ref, kt_ref, v_ref, gate_ref):
    d = x_ref.shape[-1]
    dk = d // RET_HEADS
    half = dk // 2
    hn = _rms(x_ref[...], g_ref[...]).astype(BF16)
    cos, sin = cos_ref[...], sin_ref[...]
    cos_t, sin_t = cost_ref[...], sint_ref[...]
    for h in range(RET_HEADS):
        a = slice(h * dk, h * dk + half)
        bsl = slice(h * dk + half, (h + 1) * dk)
        xh = _dot(hn, w_ref[0, :, h * dk:(h + 1) * dk])
        x1, x2 = xh[:, :half], xh[:, half:]
        q_ref[:, a] = (x1 * cos - x2 * sin).astype(BF16)
        q_ref[:, bsl] = (x1 * sin + x2 * cos).astype(BF16)
        yh = _dot_nt(wkt_ref[0, h * dk:(h + 1) * dk, :], hn)
        y1, y2 = yh[:half], yh[half:]
        kt_ref[0, a, :] = ((y1 * cos_t - y2 * sin_t) * (dk ** -0.5)).astype(BF16)
        kt_ref[0, bsl, :] = ((y1 * sin_t + y2 * cos_t) * (dk ** -0.5)).astype(BF16)
    for c0 in range(0, 2 * d, 4 * LANES):
        v_ref[:, c0:c0 + 4 * LANES] = _dot(hn, w_ref[0, :, 2 * d + c0:2 * d + c0 + 4 * LANES]).astype(BF16)
        gate_ref[:, c0:c0 + 4 * LANES] = _dot(hn, w_ref[0, :, 4 * d + c0:4 * d + c0 + 4 * LANES])


def _ret_project(j, x, g_pre, w_in, w_kt, seq):
    t, d = x.shape
    tm = TOKEN_TILE
    b = t // seq
    tiles_per_seq = seq // tm
    half = d // RET_HEADS // 2
    pos = jnp.arange(seq, dtype=F32)
    inv_freq = ROPE_BASE ** (-jnp.arange(half, dtype=F32) / half)
    ang = pos[:, None] * inv_freq[None, :]
    cos, sin = jnp.cos(ang), jnp.sin(ang)
    row = lambda i: (i, 0)
    tab = pl.BlockSpec((tm, half), lambda i: (i % tiles_per_seq, 0))
    tab_t = pl.BlockSpec((half, tm), lambda i: (0, i % tiles_per_seq))
    return pl.pallas_call(
        _ret_proj_body,
        grid=(t // tm,),
        in_specs=[
            pl.BlockSpec((tm, d), row),
            pl.BlockSpec((1, d), lambda i: (0, 0)),
            _const_spec((1, d, 6 * d), lambda i: (j, 0, 0)),
            _const_spec((1, d, d), lambda i: (j, 0, 0)),
            tab, tab, tab_t, tab_t,
        ],
        out_specs=[
            pl.BlockSpec((tm, d), row),
            pl.BlockSpec((1, d, tm), lambda i: (i // tiles_per_seq, 0, i % tiles_per_seq)),
            pl.BlockSpec((tm, 2 * d), row),
            pl.BlockSpec((tm, 2 * d), row),
        ],
        out_shape=[
            jax.ShapeDtypeStruct((t, d), BF16),
            jax.ShapeDtypeStruct((b, d, seq), BF16),
            jax.ShapeDtypeStruct((t, 2 * d), BF16),
            jax.ShapeDtypeStruct((t, 2 * d), F32),
        ],
        compiler_params=_params("arbitrary"),
        name="ret_project",
    )(x, g_pre, w_in, w_kt, cos, sin, cos.T, sin.T)


def _ret_body(q_ref, kt_ref, v_ref, gate_ref, gn_ref, intra_ref, qdec_ref, kdec_ref, cdec_ref,
              o_ref, state):
    @pl.when(pl.program_id(2) == 0)
    def _():
        state[...] = jnp.zeros_like(state)

    per_chunk = RET_CHUNK // RET_REF_CHUNK
    for sub in range(RET_STEP // RET_CHUNK):
        rows = slice(sub * RET_CHUNK, (sub + 1) * RET_CHUNK)
        q = q_ref[0, rows, :]
        kt = kt_ref[0, :, rows]
        v = v_ref[0, rows, :]
        scores = (_dot(q, kt) * intra_ref[0]).astype(BF16)
        o = _dot(scores, v) + _dot(q, state[...].astype(BF16)) * qdec_ref[0]
        kt_dec = (kt.astype(F32) * kdec_ref[0]).astype(BF16)
        state[...] = state[...] * cdec_ref[0] + _dot(kt_dec, v)

        mu = jnp.mean(o, axis=-1, keepdims=True)
        oc = o - mu
        var = jnp.mean(oc * oc, axis=-1, keepdims=True)
        on = oc * lax.rsqrt(var + GN_EPS) * gn_ref[...]
        for cl in range(per_chunk):
            c = sub * per_chunk + cl
            g = gate_ref[0, :, c, :]
            o_ref[0, :, c, :] = _silu(g) * on[cl * RET_REF_CHUNK:(cl + 1) * RET_REF_CHUNK, :]


def _ret_decay_tables(chunk):
    h = jnp.arange(RET_HEADS, dtype=F32)
    log_gamma = jnp.log1p(-jnp.exp2(-5.0 - h))
    n = jnp.arange(chunk, dtype=F32)
    diff = n[:, None] - n[None, :]
    intra = jnp.where(diff >= 0, jnp.exp(jnp.maximum(diff, 0.0) * log_gamma[:, None, None]), 0.0)
    qdec = jnp.exp((n + 1.0)[None, :] * log_gamma[:, None])[..., None]
    kdec = jnp.exp((chunk - 1.0 - n)[None, :] * log_gamma[:, None])[:, None]
    cdec = jnp.exp(chunk * log_gamma)[:, None, None]
    return intra, qdec, kdec, cdec


def _retention(q, kt, v, gate, gn):
    b, s, d = q.shape
    c = RET_CHUNK
    step = RET_STEP
    dk = d // RET_HEADS
    dv = 2 * d // RET_HEADS
    n_ref_chunks = s // RET_REF_CHUNK
    ref_chunks_per_step = step // RET_REF_CHUNK
    intra, qdec, kdec, cdec = _ret_decay_tables(c)
    head = lambda bi, h, ci: (h, 0, 0)
    permuted = pl.BlockSpec((1, RET_REF_CHUNK, ref_chunks_per_step, dv), lambda bi, h, ci: (bi, 0, ci, h))
    out = pl.pallas_call(
        _ret_body,
        grid=(b, RET_HEADS, s // step),
        in_specs=[
            pl.BlockSpec((1, step, dk), lambda bi, h, ci: (bi, ci, h)),
            pl.BlockSpec((1, dk, step), lambda bi, h, ci: (bi, h, ci)),
            pl.BlockSpec((1, step, dv), lambda bi, h, ci: (bi, ci, h)),
            permuted,
            pl.BlockSpec((1, dv), lambda bi, h, ci: (0, h)),
            pl.BlockSpec((1, c, c), head),
            pl.BlockSpec((1, c, 1), head),
            pl.BlockSpec((1, 1, c), head),
            pl.BlockSpec((1, 1, 1), head),
        ],
        out_specs=permuted,
        out_shape=jax.ShapeDtypeStruct((b, RET_REF_CHUNK, n_ref_chunks, 2 * d), F32),
        scratch_shapes=[pltpu.VMEM((dk, dv), F32)],
        compiler_params=_params("arbitrary", "arbitrary", "arbitrary"),
        name="retention",
    )(q, kt, v, gate.reshape(b, RET_REF_CHUNK, n_ref_chunks, 2 * d), gn, intra, qdec, kdec, cdec)
    return out.reshape(b * s, 2 * d)


def kernel(x, mem, norm_mix_pre, norm_mix_post, norm_xa_pre, norm_xa_post, norm_mem, norm_ffn_pre, norm_ffn_post, pool_w, pool_scale, sb_w_in, sb_w_out, ret_w_in, ret_gn, ret_w_out, xa_w_q, xa_w_kv, xa_w_o, ffn_w_in, ffn_w_out):
    b, s, d = x.shape
    depth = norm_mix_pre.shape[0]
    t = b * s
    assert s % TOKEN_TILE == 0 and s % RET_STEP == 0 and s % SB_BLOCK == 0

    bf = lambda w: w.astype(BF16)
    pool_w, sb_w_in, sb_w_out, ret_w_in, ret_w_out = map(bf, (pool_w, sb_w_in, sb_w_out, ret_w_in, ret_w_out))
    xa_w_q, xa_w_kv, xa_w_o, ffn_w_in, ffn_w_out = map(bf, (xa_w_q, xa_w_kv, xa_w_o, ffn_w_in, ffn_w_out))
    ret_w_kt = jnp.swapaxes(ret_w_in[:, :, d:2 * d], 1, 2)
    gain3 = lambda g: g.reshape(depth, 1, d)
    post_gains = tuple(map(gain3, (norm_mix_post, norm_xa_pre, norm_xa_post, norm_ffn_pre, norm_ffn_post)))

    kmem, vmem = _memory_kv(mem, norm_mem, xa_w_kv)

    xf = x.reshape(t, d)
    for i in range(depth):
        kind, j = i % N_MIXERS, i // N_MIXERS
        g_pre = norm_mix_pre[i].reshape(1, d)
        w_mix = None
        if kind == 0:
            m = _pool_mixer(j, xf.reshape(b, s, d), g_pre, pool_w, pool_scale.reshape(-1, 1, d)).reshape(t, d)
        elif kind == 1:
            q, k, v = _sb_project(j, xf, g_pre, sb_w_in)
            shape3 = lambda a: a.reshape(b, s, d)
            m = _sb_attention(shape3(q), shape3(k), shape3(v)).reshape(t, d)
            w_mix = (j, sb_w_out)
        else:
            q, kt, v, gate = _ret_project(j, xf, g_pre, ret_w_in, ret_w_kt, s)
            m = _retention(q.reshape(b, s, d), kt, v.reshape(b, s, 2 * d), gate, ret_gn[j].reshape(1, 2 * d))
            w_mix = (j, ret_w_out)
        xf = _post_layer(i, xf, m, w_mix, post_gains, kmem, vmem, xa_w_q, xa_w_o, ffn_w_in, ffn_w_out, s)
    return xf.reshape(b, s, d)
```

```python
import functools
import math

import jax
import jax.numpy as jnp
import numpy as np
from jax import lax
from jax.experimental import pallas as pl
from jax.experimental.pallas import tpu as pltpu

N_MIXERS = 3
RMS_EPS = 1e-6
GN_EPS = 1e-5
POOL_WINDOWS = (2, 4, 8, 16)
POOL_HALO = 16
POOL_PAD = 8
SB_HEADS = 16
SB_HEAD_DIM = 64
RET_HEADS = 4
ROPE_BASE = 10000.0
XA_HEADS = 4

LANES = 128
VMEM_LIMIT_BYTES = 56 * 1024 * 1024

TOKEN_TILE = 512
POST_TILE = 512
POST_PIECE_AFTER_CHUNK = (0, 2, 3, 4, 6, 7, 8)
SB_BLOCK = 128
RET_REF_CHUNK = 64
RET_CHUNK = 256
RET_STEP = 512
SB_PAIRS_PER_STEP = 8
SB_STATIC_BLOCKS = 3
SB_LOG_ZERO = -87.5

BF16 = jnp.bfloat16
F32 = jnp.float32


def _params(*sem):
    return pltpu.CompilerParams(dimension_semantics=sem, vmem_limit_bytes=VMEM_LIMIT_BYTES)


def _const_spec(shape, index_map):
    return pl.BlockSpec(shape, index_map, pipeline_mode=pl.Buffered(1))


def _rms(x, g):
    ms = jnp.mean(x * x, axis=-1, keepdims=True)
    return x * lax.rsqrt(ms + RMS_EPS) * g


def _dot(a, b):
    return jnp.dot(a, b, preferred_element_type=F32)


def _dot_nt(a, b):
    return lax.dot_general(a, b, (((1,), (1,)), ((), ())), preferred_element_type=F32)


def _silu(x):
    h = 0.5 * x
    return h + h * jnp.tanh(h)


def _kv_body(mem_ref, g_ref, w_ref, k_ref, v_ref):
    d = mem_ref.shape[-1]
    mn = _rms(mem_ref[0], g_ref[0]).astype(BF16)
    k_ref[0, 0] = _dot(mn, w_ref[0, :, :d]).astype(BF16)
    v_ref[0, 0] = _dot(mn, w_ref[0, :, d:]).astype(BF16)


def _memory_kv(mem, norm_mem, w_kv):
    b, m, d = mem.shape
    depth = w_kv.shape[0]
    out = jax.ShapeDtypeStruct((depth, b, m, d), BF16)
    return pl.pallas_call(
        _kv_body,
        grid=(depth, b),
        in_specs=[
            pl.BlockSpec((1, m, d), lambda i, j: (j, 0, 0)),
            pl.BlockSpec((1, 1, d), lambda i, j: (i, 0, 0)),
            pl.BlockSpec((1, d, 2 * d), lambda i, j: (i, 0, 0)),
        ],
        out_specs=[pl.BlockSpec((1, 1, m, d), lambda i, j: (i, j, 0, 0))] * 2,
        out_shape=[out, out],
        compiler_params=_params("arbitrary", "arbitrary"),
        name="memory_kv",
    )(mem, norm_mem.reshape(depth, 1, d), w_kv)


def _post_body(has_wmix, x_ref, m_ref, *refs):
    if has_wmix:
        wmix_ref, refs = refs[0], refs[1:]
    (g_mix, g_xa_pre, g_xa_post, g_ffn_pre, g_ffn_post, k_ref, v_ref,
     wq_ref, wo_ref, win_ref, wout_ref, o_ref, x_buf, h_buf) = refs
    d = x_ref.shape[-1]
    hd = d // XA_HEADS
    dff = wout_ref.shape[1]
    fc = 2 * LANES
    step = pl.program_id(0)
    slot_next = step % 2
    slot_now = 1 - slot_next

    @pl.when(step == 0)
    def _():
        x_buf[1] = jnp.zeros(x_buf.shape[1:], x_buf.dtype)
        h_buf[1] = jnp.zeros(h_buf.shape[1:], h_buf.dtype)

    def swiglu_chunks():
        f = None
        for c0 in range(0, dff, fc):
            h = h_buf[slot_now]
            gate = _dot(h, win_ref[0, :, c0:c0 + fc])
            up = _dot(h, win_ref[0, :, dff + c0:dff + c0 + fc])
            act = (_silu(gate) * up).astype(BF16)
            fch = _dot(act, wout_ref[0, c0:c0 + fc, :])
            f = fch if f is None else f + fch
            yield f

    def attention_pieces():
        x = x_ref[...]
        m = _dot(m_ref[...].astype(BF16), wmix_ref[0]) if has_wmix else m_ref[...]
        x = x + _rms(m, g_mix[0])
        h = _rms(x, g_xa_pre[0]).astype(BF16)
        yield
        q = _dot(h, wq_ref[0]).astype(BF16)
        yield
        heads = [slice(hh * hd, (hh + 1) * hd) for hh in range(XA_HEADS)]
        scores = [_dot_nt(q[:, sl], k_ref[0, 0, :, sl]) * (hd ** -0.5) for sl in heads]
        yield
        probs = []
        for s in scores:
            e = jnp.exp(s - jnp.max(s, axis=-1, keepdims=True))
            probs.append((e * (1.0 / jnp.sum(e, axis=-1, keepdims=True))).astype(BF16))
        yield
        outs = [_dot(p, v_ref[0, 0, :, sl]).astype(BF16) for p, sl in zip(probs, heads)]
        yield
        c = functools.reduce(lambda a, b: a + b, [_dot(oh, wo_ref[0, sl, :]) for oh, sl in zip(outs, heads)])
        yield
        x = x + _rms(c, g_xa_post[0])
        x_buf[slot_next] = x
        h_buf[slot_next] = _rms(x, g_ffn_pre[0]).astype(BF16)
        yield

    pieces = attention_pieces()
    f = None
    for n, f in enumerate(swiglu_chunks()):
        if n in POST_PIECE_AFTER_CHUNK:
            next(pieces)
    for _ in pieces:
        pass
    o_ref[...] = x_buf[slot_now] + _rms(f, g_ffn_post[0])


def _post_layer(layer, x, m, w_mix, gains, kmem, vmem, w_q, w_o, w_in, w_out, seq):
    t, d = x.shape
    tm = POST_TILE
    tiles_per_seq = seq // tm
    dff = w_out.shape[1]
    mem_len = kmem.shape[2]
    n_tiles = t // tm
    row = lambda i: (jnp.minimum(i, n_tiles - 1), 0)
    out_row = lambda i: (jnp.maximum(i - 1, 0), 0)
    lay = lambda i: (layer, 0, 0)
    in_specs = [pl.BlockSpec((tm, d), row), pl.BlockSpec((tm, m.shape[1]), row)]
    args = [x, m]
    if w_mix is not None:
        j, w_stack = w_mix
        in_specs.append(_const_spec((1,) + w_stack.shape[1:], lambda i: (j, 0, 0)))
        args.append(w_stack)
    for g in gains:
        in_specs.append(pl.BlockSpec((1, 1, d), lay))
        args.append(g)
    kv_spec = pl.BlockSpec((1, 1, mem_len, d),
                           lambda i: (layer, jnp.minimum(i, n_tiles - 1) // tiles_per_seq, 0, 0))
    in_specs += [
        kv_spec, kv_spec,
        _const_spec((1, d, d), lay), _const_spec((1, d, d), lay),
        _const_spec((1, d, 2 * dff), lay), _const_spec((1, dff, d), lay),
    ]
    args += [kmem, vmem, w_q, w_o, w_in, w_out]
    return pl.pallas_call(
        functools.partial(_post_body, w_mix is not None),
        grid=(n_tiles + 1,),
        in_specs=in_specs,
        out_specs=pl.BlockSpec((tm, d), out_row),
        out_shape=jax.ShapeDtypeStruct((t, d), F32),
        scratch_shapes=[pltpu.VMEM((2, tm, d), F32), pltpu.VMEM((2, tm, d), BF16)],
        compiler_params=_params("arbitrary"),
        name="post_mixer",
    )(*args)


def _pool_body(x_ref, g_ref, w_ref, scale_ref, o_ref, hbuf, sbuf):
    tm, d = x_ref.shape[1], x_ref.shape[2]
    gd = d // len(POOL_WINDOWS)
    s_idx = pl.program_id(1)
    lo, mid, hi = POOL_PAD, POOL_PAD + POOL_HALO, POOL_PAD + POOL_HALO + tm

    @pl.when(s_idx == 0)
    def _():
        hbuf[0:mid, :] = jnp.zeros((mid, d), F32)
        sbuf[0:lo, :] = jnp.zeros((lo, gd), F32)

    @pl.when(s_idx != 0)
    def _():
        hbuf[lo:mid, :] = hbuf[hi - POOL_HALO:hi, :]

    hn = _rms(x_ref[0], g_ref[...])
    hbuf[mid:hi, :] = hn

    t = s_idx * tm + lax.broadcasted_iota(jnp.int32, (tm, 1), 0)
    for g, w in enumerate(POOL_WINDOWS):
        cols = slice(g * gd, (g + 1) * gd)
        acc = hbuf[lo:hi, cols] + hbuf[lo - 1:hi - 1, cols]
        shift = 2
        while shift < w:
            sbuf[lo:hi, :] = acc
            acc = acc + sbuf[lo - shift:hi - shift, :]
            shift *= 2
        inv = 1.0 / jnp.minimum(t + 1, w).astype(F32)
        dlt = (acc[POOL_HALO:] * inv - hn[:, cols]).astype(BF16)
        o_ref[0, :, cols] = _dot(dlt, w_ref[0, g]) * scale_ref[0, :, cols]


def _pool_mixer(j, x3, g_pre, pool_w, pool_scale):
    b, s, d = x3.shape
    tm = TOKEN_TILE
    ng, gd = pool_w.shape[1], pool_w.shape[2]
    return pl.pallas_call(
        _pool_body,
        grid=(b, s // tm),
        in_specs=[
            pl.BlockSpec((1, tm, d), lambda i, k: (i, k, 0)),
            pl.BlockSpec((1, d), lambda i, k: (0, 0)),
            pl.BlockSpec((1, ng, gd, gd), lambda i, k: (j, 0, 0, 0)),
            pl.BlockSpec((1, 1, d), lambda i, k: (j, 0, 0)),
        ],
        out_specs=pl.BlockSpec((1, tm, d), lambda i, k: (i, k, 0)),
        out_shape=jax.ShapeDtypeStruct((b, s, d), F32),
        scratch_shapes=[pltpu.VMEM((POOL_PAD + POOL_HALO + tm, d), F32),
                        pltpu.VMEM((POOL_PAD + POOL_HALO + tm, gd), F32)],
        compiler_params=_params("arbitrary", "arbitrary"),
        name="pool_mixer",
    )(x3, g_pre, pool_w, pool_scale)


def _sb_proj_body(x_ref, g_ref, w_ref, q_ref, k_ref, v_ref):
    d = x_ref.shape[-1]
    hn = _rms(x_ref[...], g_ref[...]).astype(BF16)
    q_ref[...] = (_dot(hn, w_ref[0, :, :d]) * (SB_HEAD_DIM ** -0.5)).astype(BF16)
    k_ref[...] = _dot(hn, w_ref[0, :, d:2 * d]).astype(BF16)
    v_ref[...] = _dot(hn, w_ref[0, :, 2 * d:]).astype(BF16)


def _sb_project(j, x, g_pre, w_in):
    t, d = x.shape
    tm = TOKEN_TILE
    row = lambda i: (i, 0)
    out = jax.ShapeDtypeStruct((t, d), BF16)
    return pl.pallas_call(
        _sb_proj_body,
        grid=(t // tm,),
        in_specs=[
            pl.BlockSpec((tm, d), row),
            pl.BlockSpec((1, d), lambda i: (0, 0)),
            _const_spec((1, d, 3 * d), lambda i: (j, 0, 0)),
        ],
        out_specs=[pl.BlockSpec((tm, d), row)] * 3,
        out_shape=[out, out, out],
        compiler_params=_params("arbitrary"),
        name="sb_project",
    )(x, g_pre, w_in)


def _sb_attn_body(q_ref, k_ref, v_ref, cum_ref, o_ref):
    tq = q_ref.shape[1]
    n_pairs = q_ref.shape[2] // LANES
    i = pl.program_id(2)
    first = lax.broadcasted_iota(jnp.int32, (tq, LANES), 1) < SB_HEAD_DIM
    row = lax.broadcasted_iota(jnp.int32, (2 * tq, SB_BLOCK), 0) & (tq - 1)
    col = lax.broadcasted_iota(jnp.int32, (2 * tq, SB_BLOCK), 1)
    diagonal = col < row
    cum = cum_ref[...]

    def stacked_queries(p):
        q = q_ref[0, :, p * LANES:(p + 1) * LANES]
        zero = jnp.zeros_like(q)
        return jnp.concatenate([jnp.where(first, q, zero), jnp.where(first, zero, q)], axis=0)

    def rows_of(ref, p, j):
        start = pl.multiple_of(j * SB_BLOCK, SB_BLOCK)
        return ref[0, pl.ds(start, SB_BLOCK), p * LANES:(p + 1) * LANES]

    def walk(tasks, q2s, rs, accs):
        zs = [_dot_nt(q2s[p], rows_of(k_ref, p, j)) for p, j, _, _ in tasks]
        ts = []
        for z, (_, _, mask, _) in zip(zs, tasks):
            drop = jnp.maximum(z, 0.0) + jnp.log(1.0 + jnp.exp(-jnp.abs(z)))
            dm = drop if mask is None else jnp.where(mask, drop, 0.0)
            hi = dm.astype(BF16)
            lo = (dm - hi.astype(F32)).astype(BF16)
            ts.append(_dot(jnp.concatenate([hi, lo], axis=1), cum))
        rs, accs = list(rs), list(accs)
        for z, (p, j, mask, exists), t in zip(zs, tasks, ts):
            a = jnp.exp(z - (t[:, :SB_BLOCK] + rs[p]))
            if mask is not None:
                a = jnp.where(mask, a, 0.0)
            vj = rows_of(v_ref, p, j)
            if exists is not None:
                vj = jnp.where(exists, vj, jnp.zeros_like(vj))
            accs[p] = accs[p] + _dot(a.astype(BF16), vj)
            rs[p] = rs[p] + t[:, SB_BLOCK:]
        return rs, accs

    zeros = jnp.zeros((2 * tq, LANES), F32)
    q2s = [stacked_queries(p) for p in range(n_pairs)]
    tasks = []
    for p in range(n_pairs):
        tasks.append((p, i, diagonal, None))
        for u in range(1, SB_STATIC_BLOCKS):
            tasks.append((p, jnp.maximum(i - u, 0), None, i - u >= 0))
    rs, accs = walk(tasks, q2s, [zeros] * n_pairs, [zeros] * n_pairs)

    def more_to_come(carry):
        j, rs, _ = carry
        return jnp.logical_and(j >= 0, jnp.min(functools.reduce(jnp.minimum, rs)) < -SB_LOG_ZERO)

    def one_more_block(carry):
        j, rs, accs = carry
        rs, accs = walk([(p, j, None, None) for p in range(n_pairs)], q2s, rs, accs)
        return j - 1, rs, accs

    _, _, accs = lax.while_loop(more_to_come, one_more_block, (i - SB_STATIC_BLOCKS, rs, accs))
    for p, acc in enumerate(accs):
        o_ref[0, :, p * LANES:(p + 1) * LANES] = jnp.where(first, acc[:tq], acc[tq:]).astype(BF16)


def _sb_cumsum_matrix():
    n = SB_BLOCK
    tri = (np.arange(n)[:, None] >= np.arange(n)[None, :]).astype(np.float32)
    half = np.concatenate([tri, np.ones((n, n), np.float32)], axis=1)
    return jnp.asarray(np.concatenate([half, half], axis=0), dtype=BF16)


def _sb_attention(q, k, v):
    b, s, d = q.shape
    tq = SB_BLOCK
    width = SB_PAIRS_PER_STEP * LANES
    blk = lambda bi, hp, i: (bi, i, hp)
    full = lambda bi, hp, i: (bi, 0, hp)
    return pl.pallas_call(
        _sb_attn_body,
        grid=(b, d // width, s // tq),
        in_specs=[
            pl.BlockSpec((1, tq, width), blk),
            pl.BlockSpec((1, s, width), full),
            pl.BlockSpec((1, s, width), full),
            pl.BlockSpec((2 * SB_BLOCK, 2 * SB_BLOCK), lambda bi, hp, i: (0, 0)),
        ],
        out_specs=pl.BlockSpec((1, tq, width), blk),
        out_shape=jax.ShapeDtypeStruct((b, s, d), BF16),
        compiler_params=_params("arbitrary", "arbitrary", "arbitrary"),
        name="sb_attention",
    )(q, k, v, _sb_cumsum_matrix())


def _ret_proj_body(x_ref, g_ref, w_ref, wkt_ref, cos_ref, sin_ref, cost_ref, sint_ref,
                   q_ref, kt_ref, v_ref, gate_ref):
    d = x_ref.shape[-1]
    dk = d // RET_HEADS
    half = dk // 2
    hn = _rms(x_ref[...], g_ref[...]).astype(BF16)
    cos, sin = cos_ref[...], sin_ref[...]
    cos_t, sin_t = cost_ref[...], sint_ref[...]
    for h in range(RET_HEADS):
        a = slice(h * dk, h * dk + half)
        bsl = slice(h * dk + half, (h + 1) * dk)
        xh = _dot(hn, w_ref[0, :, h * dk:(h + 1) * dk])
        x1, x2 = xh[:, :half], xh[:, half:]
        q_ref[:, a] = (x1 * cos - x2 * sin).astype(BF16)
        q_ref[:, bsl] = (x1 * sin + x2 * cos).astype(BF16)
        yh = _dot_nt(wkt_ref[0, h * dk:(h + 1) * dk, :], hn)
        y1, y2 = yh[:half], yh[half:]
        kt_ref[0, a, :] = ((y1 * cos_t - y2 * sin_t) * (dk ** -0.5)).astype(BF16)
        kt_ref[0, bsl, :] = ((y1 * sin_t + y2 * cos_t) * (dk ** -0.5)).astype(BF16)
    for c0 in range(0, 2 * d, 4 * LANES):
        v_ref[:, c0:c0 + 4 * LANES] = _dot(hn, w_ref[0, :, 2 * d + c0:2 * d + c0 + 4 * LANES]).astype(BF16)
        gate_ref[:, c0:c0 + 4 * LANES] = _dot(hn, w_ref[0, :, 4 * d + c0:4 * d + c0 + 4 * LANES])


def _ret_project(j, x, g_pre, w_in, w_kt, seq):
    t, d = x.shape
    tm = TOKEN_TILE
    b = t // seq
    tiles_per_seq = seq // tm
    half = d // RET_HEADS // 2
    pos = jnp.arange(seq, dtype=F32)
    inv_freq = ROPE_BASE ** (-jnp.arange(half, dtype=F32) / half)
    ang = pos[:, None] * inv_freq[None, :]
    cos, sin = jnp.cos(ang), jnp.sin(ang)
    row = lambda i: (i, 0)
    tab = pl.BlockSpec((tm, half), lambda i: (i % tiles_per_seq, 0))
    tab_t = pl.BlockSpec((half, tm), lambda i: (0, i % tiles_per_seq))
    return pl.pallas_call(
        _ret_proj_body,
        grid=(t // tm,),
        in_specs=[
            pl.BlockSpec((tm, d), row),
            pl.BlockSpec((1, d), lambda i: (0, 0)),
            _const_spec((1, d, 6 * d), lambda i: (j, 0, 0)),
            _const_spec((1, d, d), lambda i: (j, 0, 0)),
            tab, tab, tab_t, tab_t,
        ],
        out_specs=[
            pl.BlockSpec((tm, d), row),
            pl.BlockSpec((1, d, tm), lambda i: (i // tiles_per_seq, 0, i % tiles_per_seq)),
            pl.BlockSpec((tm, 2 * d), row),
            pl.BlockSpec((tm, 2 * d), row),
        ],
        out_shape=[
            jax.ShapeDtypeStruct((t, d), BF16),
            jax.ShapeDtypeStruct((b, d, seq), BF16),
            jax.ShapeDtypeStruct((t, 2 * d), BF16),
            jax.ShapeDtypeStruct((t, 2 * d), F32),
        ],
        compiler_params=_params("arbitrary"),
        name="ret_project",
    )(x, g_pre, w_in, w_kt, cos, sin, cos.T, sin.T)


def _ret_body(q_ref, kt_ref, v_ref, gate_ref, gn_ref, intra_ref, qdec_ref, kdec_ref, cdec_ref,
              o_ref, state):
    @pl.when(pl.program_id(2) == 0)
    def _():
        state[...] = jnp.zeros_like(state)

    per_chunk = RET_CHUNK // RET_REF_CHUNK
    for sub in range(RET_STEP // RET_CHUNK):
        rows = slice(sub * RET_CHUNK, (sub + 1) * RET_CHUNK)
        q = q_ref[0, rows, :]
        kt = kt_ref[0, :, rows]
        v = v_ref[0, rows, :]
        scores = (_dot(q, kt) * intra_ref[0]).astype(BF16)
        o = _dot(scores, v) + _dot(q, state[...].astype(BF16)) * qdec_ref[0]
        kt_dec = (kt.astype(F32) * kdec_ref[0]).astype(BF16)
        state[...] = state[...] * cdec_ref[0] + _dot(kt_dec, v)

        mu = jnp.mean(o, axis=-1, keepdims=True)
        oc = o - mu
        var = jnp.mean(oc * oc, axis=-1, keepdims=True)
        on = oc * lax.rsqrt(var + GN_EPS) * gn_ref[...]
        for cl in range(per_chunk):
            c = sub * per_chunk + cl
            g = gate_ref[0, :, c, :]
            o_ref[0, :, c, :] = _silu(g) * on[cl * RET_REF_CHUNK:(cl + 1) * RET_REF_CHUNK, :]


def _ret_decay_tables(chunk):
    h = jnp.arange(RET_HEADS, dtype=F32)
    log_gamma = jnp.log1p(-jnp.exp2(-5.0 - h))
    n = jnp.arange(chunk, dtype=F32)
    diff = n[:, None] - n[None, :]
    intra = jnp.where(diff >= 0, jnp.exp(jnp.maximum(diff, 0.0) * log_gamma[:, None, None]), 0.0)
    qdec = jnp.exp((n + 1.0)[None, :] * log_gamma[:, None])[..., None]
    kdec = jnp.exp((chunk - 1.0 - n)[None, :] * log_gamma[:, None])[:, None]
    cdec = jnp.exp(chunk * log_gamma)[:, None, None]
    return intra, qdec, kdec, cdec


def _retention(q, kt, v, gate, gn):
    b, s, d = q.shape
    c = RET_CHUNK
    step = RET_STEP
    dk = d // RET_HEADS
    dv = 2 * d // RET_HEADS
    n_ref_chunks = s // RET_REF_CHUNK
    ref_chunks_per_step = step // RET_REF_CHUNK
    intra, qdec, kdec, cdec = _ret_decay_tables(c)
    head = lambda bi, h, ci: (h, 0, 0)
    permuted = pl.BlockSpec((1, RET_REF_CHUNK, ref_chunks_per_step, dv), lambda bi, h, ci: (bi, 0, ci, h))
    out = pl.pallas_call(
        _ret_body,
        grid=(b, RET_HEADS, s // step),
        in_specs=[
            pl.BlockSpec((1, step, dk), lambda bi, h, ci: (bi, ci, h)),
            pl.BlockSpec((1, dk, step), lambda bi, h, ci: (bi, h, ci)),
            pl.BlockSpec((1, step, dv), lambda bi, h, ci: (bi, ci, h)),
            permuted,
            pl.BlockSpec((1, dv), lambda bi, h, ci: (0, h)),
            pl.BlockSpec((1, c, c), head),
            pl.BlockSpec((1, c, 1), head),
            pl.BlockSpec((1, 1, c), head),
            pl.BlockSpec((1, 1, 1), head),
        ],
        out_specs=permuted,
        out_shape=jax.ShapeDtypeStruct((b, RET_REF_CHUNK, n_ref_chunks, 2 * d), F32),
        scratch_shapes=[pltpu.VMEM((dk, dv), F32)],
        compiler_params=_params("arbitrary", "arbitrary", "arbitrary"),
        name="retention",
    )(q, kt, v, gate.reshape(b, RET_REF_CHUNK, n_ref_chunks, 2 * d), gn, intra, qdec, kdec, cdec)
    return out.reshape(b * s, 2 * d)


def kernel(x, mem, norm_mix_pre, norm_mix_post, norm_xa_pre, norm_xa_post, norm_mem, norm_ffn_pre, norm_ffn_post, pool_w, pool_scale, sb_w_in, sb_w_out, ret_w_in, ret_gn, ret_w_out, xa_w_q, xa_w_kv, xa_w_o, ffn_w_in, ffn_w_out):
    b, s, d = x.shape
    depth = norm_mix_pre.shape[0]
    t = b * s
    assert s % TOKEN_TILE == 0 and s % RET_STEP == 0 and s % SB_BLOCK == 0

    bf = lambda w: w.astype(BF16)
    pool_w, sb_w_in, sb_w_out, ret_w_in, ret_w_out = map(bf, (pool_w, sb_w_in, sb_w_out, ret_w_in, ret_w_out))
    xa_w_q, xa_w_kv, xa_w_o, ffn_w_in, ffn_w_out = map(bf, (xa_w_q, xa_w_kv, xa_w_o, ffn_w_in, ffn_w_out))
    ret_w_kt = jnp.swapaxes(ret_w_in[:, :, d:2 * d], 1, 2)
    gain3 = lambda g: g.reshape(depth, 1, d)
    post_gains = tuple(map(gain3, (norm_mix_post, norm_xa_pre, norm_xa_post, norm_ffn_pre, norm_ffn_post)))

    kmem, vmem = _memory_kv(mem, norm_mem, xa_w_kv)

    xf = x.reshape(t, d)
    for i in range(depth):
        kind, j = i % N_MIXERS, i // N_MIXERS
        g_pre = norm_mix_pre[i].reshape(1, d)
        w_mix = None
        if kind == 0:
            m = _pool_mixer(j, xf.reshape(b, s, d), g_pre, pool_w, pool_scale.reshape(-1, 1, d)).reshape(t, d)
        elif kind == 1:
            q, k, v = _sb_project(j, xf, g_pre, sb_w_in)
            shape3 = lambda a: a.reshape(b, s, d)
            m = _sb_attention(shape3(q), shape3(k), shape3(v)).reshape(t, d)
            w_mix = (j, sb_w_out)
        else:
            q, kt, v, gate = _ret_project(j, xf, g_pre, ret_w_in, ret_w_kt, s)
            m = _retention(q.reshape(b, s, d), kt, v.reshape(b, s, 2 * d), gate, ret_gn[j].reshape(1, 2 * d))
            w_mix = (j, ret_w_out)
        xf = _post_layer(i, xf, m, w_mix, post_gains, kmem, vmem, xa_w_q, xa_w_o, ffn_w_in, ffn_w_out, s)
    return xf.reshape(b, s, d)
```

```python
import functools
import math

import jax
import jax.numpy as jnp
import numpy as np
from jax import lax
from jax.experimental import pallas as pl
from jax.experimental.pallas import tpu as pltpu

N_MIXERS = 3
RMS_EPS = 1e-6
GN_EPS = 1e-5
POOL_WINDOWS = (2, 4, 8, 16)
POOL_HALO = 16
POOL_PAD = 8
SB_HEADS = 16
SB_HEAD_DIM = 64
RET_HEADS = 4
ROPE_BASE = 10000.0
XA_HEADS = 4

LANES = 128
VMEM_LIMIT_BYTES = 56 * 1024 * 1024

TOKEN_TILE = 512
POST_TILE = 512
POST_PIECE_AFTER_CHUNK = (0, 2, 3, 4, 6, 7, 8)
SB_BLOCK = 128
RET_REF_CHUNK = 64
RET_CHUNK = 256
RET_STEP = 512
SB_PAIRS_PER_STEP = 8
SB_STATIC_BLOCKS = 3
SB_LOG_ZERO = -87.5

BF16 = jnp.bfloat16
F32 = jnp.float32


def _params(*sem):
    return pltpu.CompilerParams(dimension_semantics=sem, vmem_limit_bytes=VMEM_LIMIT_BYTES)


def _const_spec(shape, index_map):
    return pl.BlockSpec(shape, index_map, pipeline_mode=pl.Buffered(1))


def _rms(x, g):
    ms = jnp.mean(x * x, axis=-1, keepdims=True)
    return x * lax.rsqrt(ms + RMS_EPS) * g


def _dot(a, b):
    return jnp.dot(a, b, preferred_element_type=F32)


def _dot_nt(a, b):
    return lax.dot_general(a, b, (((1,), (1,)), ((), ())), preferred_element_type=F32)


def _silu(x):
    h = 0.5 * x
    return h + h * jnp.tanh(h)


def _kv_body(mem_ref, g_ref, w_ref, k_ref, v_ref):
    d = mem_ref.shape[-1]
    mn = _rms(mem_ref[0], g_ref[0]).astype(BF16)
    k_ref[0, 0] = _dot(mn, w_ref[0, :, :d]).astype(BF16)
    v_ref[0, 0] = _dot(mn, w_ref[0, :, d:]).astype(BF16)


def _memory_kv(mem, norm_mem, w_kv):
    b, m, d = mem.shape
    depth = w_kv.shape[0]
    out = jax.ShapeDtypeStruct((depth, b, m, d), BF16)
    return pl.pallas_call(
        _kv_body,
        grid=(depth, b),
        in_specs=[
            pl.BlockSpec((1, m, d), lambda i, j: (j, 0, 0)),
            pl.BlockSpec((1, 1, d), lambda i, j: (i, 0, 0)),
            pl.BlockSpec((1, d, 2 * d), lambda i, j: (i, 0, 0)),
        ],
        out_specs=[pl.BlockSpec((1, 1, m, d), lambda i, j: (i, j, 0, 0))] * 2,
        out_shape=[out, out],
        compiler_params=_params("arbitrary", "arbitrary"),
        name="memory_kv",
    )(mem, norm_mem.reshape(depth, 1, d), w_kv)


def _post_body(has_wmix, x_ref, m_ref, *refs):
    if has_wmix:
        wmix_ref, refs = refs[0], refs[1:]
    (g_mix, g_xa_pre, g_xa_post, g_ffn_pre, g_ffn_post, k_ref, v_ref,
     wq_ref, wo_ref, win_ref, wout_ref, o_ref, x_buf, h_buf) = refs
    d = x_ref.shape[-1]
    hd = d // XA_HEADS
    dff = wout_ref.shape[1]
    fc = 2 * LANES
    step = pl.program_id(0)
    slot_next = step % 2
    slot_now = 1 - slot_next

    @pl.when(step == 0)
    def _():
        x_buf[1] = jnp.zeros(x_buf.shape[1:], x_buf.dtype)
        h_buf[1] = jnp.zeros(h_buf.shape[1:], h_buf.dtype)

    def swiglu_chunks():
        f = None
        for c0 in range(0, dff, fc):
            h = h_buf[slot_now]
            gate = _dot(h, win_ref[0, :, c0:c0 + fc])
            up = _dot(h, win_ref[0, :, dff + c0:dff + c0 + fc])
            act = (_silu(gate) * up).astype(BF16)
            fch = _dot(act, wout_ref[0, c0:c0 + fc, :])
            f = fch if f is None else f + fch
            yield f

    def attention_pieces():
        x = x_ref[...]
        m = _dot(m_ref[...].astype(BF16), wmix_ref[0]) if has_wmix else m_ref[...]
        x = x + _rms(m, g_mix[0])
        h = _rms(x, g_xa_pre[0]).astype(BF16)
        yield
        q = _dot(h, wq_ref[0]).astype(BF16)
        yield
        heads = [slice(hh * hd, (hh + 1) * hd) for hh in range(XA_HEADS)]
        scores = [_dot_nt(q[:, sl], k_ref[0, 0, :, sl]) * (hd ** -0.5) for sl in heads]
        yield
        probs = []
        for s in scores:
            e = jnp.exp(s - jnp.max(s, axis=-1, keepdims=True))
            probs.append((e * (1.0 / jnp.sum(e, axis=-1, keepdims=True))).astype(BF16))
        yield
        outs = [_dot(p, v_ref[0, 0, :, sl]).astype(BF16) for p, sl in zip(probs, heads)]
        yield
        c = functools.reduce(lambda a, b: a + b, [_dot(oh, wo_ref[0, sl, :]) for oh, sl in zip(outs, heads)])
        yield
        x = x + _rms(c, g_xa_post[0])
        x_buf[slot_next] = x
        h_buf[slot_next] = _rms(x, g_ffn_pre[0]).astype(BF16)
        yield

    pieces = attention_pieces()
    f = None
    for n, f in enumerate(swiglu_chunks()):
        if n in POST_PIECE_AFTER_CHUNK:
            next(pieces)
    for _ in pieces:
        pass
    o_ref[...] = x_buf[slot_now] + _rms(f, g_ffn_post[0])


def _post_layer(layer, x, m, w_mix, gains, kmem, vmem, w_q, w_o, w_in, w_out, seq):
    t, d = x.shape
    tm = POST_TILE
    tiles_per_seq = seq // tm
    dff = w_out.shape[1]
    mem_len = kmem.shape[2]
    n_tiles = t // tm
    row = lambda i: (jnp.minimum(i, n_tiles - 1), 0)
    out_row = lambda i: (jnp.maximum(i - 1, 0), 0)
    lay = lambda i: (layer, 0, 0)
    in_specs = [pl.BlockSpec((tm, d), row), pl.BlockSpec((tm, m.shape[1]), row)]
    args = [x, m]
    if w_mix is not None:
        j, w_stack = w_mix
        in_specs.append(_const_spec((1,) + w_stack.shape[1:], lambda i: (j, 0, 0)))
        args.append(w_stack)
    for g in gains:
        in_specs.append(pl.BlockSpec((1, 1, d), lay))
        args.append(g)
    kv_spec = pl.BlockSpec((1, 1, mem_len, d),
                           lambda i: (layer, jnp.minimum(i, n_tiles - 1) // tiles_per_seq, 0, 0))
    in_specs += [
        kv_spec, kv_spec,
        _const_spec((1, d, d), lay), _const_spec((1, d, d), lay),
        _const_spec((1, d, 2 * dff), lay), _const_spec((1, dff, d), lay),
    ]
    args += [kmem, vmem, w_q, w_o, w_in, w_out]
    return pl.pallas_call(
        functools.partial(_post_body, w_mix is not None),
        grid=(n_tiles + 1,),
        in_specs=in_specs,
        out_specs=pl.BlockSpec((tm, d), out_row),
        out_shape=jax.ShapeDtypeStruct((t, d), F32),
        scratch_shapes=[pltpu.VMEM((2, tm, d), F32), pltpu.VMEM((2, tm, d), BF16)],
        compiler_params=_params("arbitrary"),
        name="post_mixer",
    )(*args)


def _pool_body(x_ref, g_ref, w_ref, scale_ref, o_ref, hbuf, sbuf):
    tm, d = x_ref.shape[1], x_ref.shape[2]
    gd = d // len(POOL_WINDOWS)
    s_idx = pl.program_id(1)
    lo, mid, hi = POOL_PAD, POOL_PAD + POOL_HALO, POOL_PAD + POOL_HALO + tm

    @pl.when(s_idx == 0)
    def _():
        hbuf[0:mid, :] = jnp.zeros((mid, d), F32)
        sbuf[0:lo, :] = jnp.zeros((lo, gd), F32)

    @pl.when(s_idx != 0)
    def _():
        hbuf[lo:mid, :] = hbuf[hi - POOL_HALO:hi, :]

    hn = _rms(x_ref[0], g_ref[...])
    hbuf[mid:hi, :] = hn

    t = s_idx * tm + lax.broadcasted_iota(jnp.int32, (tm, 1), 0)
    for g, w in enumerate(POOL_WINDOWS):
        cols = slice(g * gd, (g + 1) * gd)
        acc = hbuf[lo:hi, cols] + hbuf[lo - 1:hi - 1, cols]
        shift = 2
        while shift < w:
            sbuf[lo:hi, :] = acc
            acc = acc + sbuf[lo - shift:hi - shift, :]
            shift *= 2
        inv = 1.0 / jnp.minimum(t + 1, w).astype(F32)
        dlt = (acc[POOL_HALO:] * inv - hn[:, cols]).astype(BF16)
        o_ref[0, :, cols] = _dot(dlt, w_ref[0, g]) * scale_ref[0, :, cols]


def _pool_mixer(j, x3, g_pre, pool_w, pool_scale):
    b, s, d = x3.shape
    tm = TOKEN_TILE
    ng, gd = pool_w.shape[1], pool_w.shape[2]
    return pl.pallas_call(
        _pool_body,
        grid=(b, s // tm),
        in_specs=[
            pl.BlockSpec((1, tm, d), lambda i, k: (i, k, 0)),
            pl.BlockSpec((1, d), lambda i, k: (0, 0)),
            pl.BlockSpec((1, ng, gd, gd), lambda i, k: (j, 0, 0, 0)),
            pl.BlockSpec((1, 1, d), lambda i, k: (j, 0, 0)),
        ],
        out_specs=pl.BlockSpec((1, tm, d), lambda i, k: (i, k, 0)),
        out_shape=jax.ShapeDtypeStruct((b, s, d), F32),
        scratch_shapes=[pltpu.VMEM((POOL_PAD + POOL_HALO + tm, d), F32),
                        pltpu.VMEM((POOL_PAD + POOL_HALO + tm, gd), F32)],
        compiler_params=_params("arbitrary", "arbitrary"),
        name="pool_mixer",
    )(x3, g_pre, pool_w, pool_scale)


def _sb_proj_body(x_ref, g_ref, w_ref, q_ref, k_ref, v_ref):
    d = x_ref.shape[-1]
    hn = _rms(x_ref[...], g_ref[...]).astype(BF16)
    q_ref[...] = (_dot(hn, w_ref[0, :, :d]) * (SB_HEAD_DIM ** -0.5)).astype(BF16)
    k_ref[...] = _dot(hn, w_ref[0, :, d:2 * d]).astype(BF16)
    v_ref[...] = _dot(hn, w_ref[0, :, 2 * d:]).astype(BF16)


def _sb_project(j, x, g_pre, w_in):
    t, d = x.shape
    tm = TOKEN_TILE
    row = lambda i: (i, 0)
    out = jax.ShapeDtypeStruct((t, d), BF16)
    return pl.pallas_call(
        _sb_proj_body,
        grid=(t // tm,),
        in_specs=[
            pl.BlockSpec((tm, d), row),
            pl.BlockSpec((1, d), lambda i: (0, 0)),
            _const_spec((1, d, 3 * d), lambda i: (j, 0, 0)),
        ],
        out_specs=[pl.BlockSpec((tm, d), row)] * 3,
        out_shape=[out, out, out],
        compiler_params=_params("arbitrary"),
        name="sb_project",
    )(x, g_pre, w_in)


def _sb_attn_body(q_ref, k_ref, v_ref, cum_ref, o_ref):
    tq = q_ref.shape[1]
    n_pairs = q_ref.shape[2] // LANES
    i = pl.program_id(2)
    first = lax.broadcasted_iota(jnp.int32, (tq, LANES), 1) < SB_HEAD_DIM
    row = lax.broadcasted_iota(jnp.int32, (2 * tq, SB_BLOCK), 0) & (tq - 1)
    col = lax.broadcasted_iota(jnp.int32, (2 * tq, SB_BLOCK), 1)
    diagonal = col < row
    cum = cum_ref[...]

    def stacked_queries(p):
        q = q_ref[0, :, p * LANES:(p + 1) * LANES]
        zero = jnp.zeros_like(q)
        return jnp.concatenate([jnp.where(first, q, zero), jnp.where(first, zero, q)], axis=0)

    def rows_of(ref, p, j):
        start = pl.multiple_of(j * SB_BLOCK, SB_BLOCK)
        return ref[0, pl.ds(start, SB_BLOCK), p * LANES:(p + 1) * LANES]

    def walk(tasks, q2s, rs, accs):
        zs = [_dot_nt(q2s[p], rows_of(k_ref, p, j)) for p, j, _, _ in tasks]
        ts = []
        for z, (_, _, mask, _) in zip(zs, tasks):
            drop = jnp.maximum(z, 0.0) + jnp.log(1.0 + jnp.exp(-jnp.abs(z)))
            dm = drop if mask is None else jnp.where(mask, drop, 0.0)
            hi = dm.astype(BF16)
            lo = (dm - hi.astype(F32)).astype(BF16)
            ts.append(_dot(jnp.concatenate([hi, lo], axis=1), cum))
        rs, accs = list(rs), list(accs)
        for z, (p, j, mask, exists), t in zip(zs, tasks, ts):
            a = jnp.exp(z - (t[:, :SB_BLOCK] + rs[p]))
            if mask is not None:
                a = jnp.where(mask, a, 0.0)
            vj = rows_of(v_ref, p, j)
            if exists is not None:
                vj = jnp.where(exists, vj, jnp.zeros_like(vj))
            accs[p] = accs[p] + _dot(a.astype(BF16), vj)
            rs[p] = rs[p] + t[:, SB_BLOCK:]
        return rs, accs

    zeros = jnp.zeros((2 * tq, LANES), F32)
    q2s = [stacked_queries(p) for p in range(n_pairs)]
    tasks = []
    for p in range(n_pairs):
        tasks.append((p, i, diagonal, None))
        for u in range(1, SB_STATIC_BLOCKS):
            tasks.append((p, jnp.maximum(i - u, 0), None, i - u >= 0))
    rs, accs = walk(tasks, q2s, [zeros] * n_pairs, [zeros] * n_pairs)

    def more_to_come(carry):
        j, rs, _ = carry
        return jnp.logical_and(j >= 0, jnp.min(functools.reduce(jnp.minimum, rs)) < -SB_LOG_ZERO)

    def one_more_block(carry):
        j, rs, accs = carry
        rs, accs = walk([(p, j, None, None) for p in range(n_pairs)], q2s, rs, accs)
        return j - 1, rs, accs

    _, _, accs = lax.while_loop(more_to_come, one_more_block, (i - SB_STATIC_BLOCKS, rs, accs))
    for p, acc in enumerate(accs):
        o_ref[0, :, p * LANES:(p + 1) * LANES] = jnp.where(first, acc[:tq], acc[tq:]).astype(BF16)


def _sb_cumsum_matrix():
    n = SB_BLOCK
    tri = (np.arange(n)[:, None] >= np.arange(n)[None, :]).astype(np.float32)
    half = np.concatenate([tri, np.ones((n, n), np.float32)], axis=1)
    return jnp.asarray(np.concatenate([half, half], axis=0), dtype=BF16)


def _sb_attention(q, k, v):
    b, s, d = q.shape
    tq = SB_BLOCK
    width = SB_PAIRS_PER_STEP * LANES
    blk = lambda bi, hp, i: (bi, i, hp)
    full = lambda bi, hp, i: (bi, 0, hp)
    return pl.pallas_call(
        _sb_attn_body,
        grid=(b, d // width, s // tq),
        in_specs=[
            pl.BlockSpec((1, tq, width), blk),
            pl.BlockSpec((1, s, width), full),
            pl.BlockSpec((1, s, width), full),
            pl.BlockSpec((2 * SB_BLOCK, 2 * SB_BLOCK), lambda bi, hp, i: (0, 0)),
        ],
        out_specs=pl.BlockSpec((1, tq, width), blk),
        out_shape=jax.ShapeDtypeStruct((b, s, d), BF16),
        compiler_params=_params("arbitrary", "arbitrary", "arbitrary"),
        name="sb_attention",
    )(q, k, v, _sb_cumsum_matrix())


def _ret_proj_body(x_ref, g_ref, w_ref, wkt_ref, cos_ref, sin_ref, cost_ref, sint_ref,
                   q_ref, kt_ref, v_ref, gate_ref):
    d = x_ref.shape[-1]
    dk = d // RET_HEADS
    half = dk // 2
    hn = _rms(x_ref[...], g_ref[...]).astype(BF16)
    cos, sin = cos_ref[...], sin_ref[...]
    cos_t, sin_t = cost_ref[...], sint_ref[...]
    for h in range(RET_HEADS):
        a = slice(h * dk, h * dk + half)
        bsl = slice(h * dk + half, (h + 1) * dk)
        xh = _dot(hn, w_ref[0, :, h * dk:(h + 1) * dk])
        x1, x2 = xh[:, :half], xh[:, half:]
        q_ref[:, a] = (x1 * cos - x2 * sin).astype(BF16)
        q_ref[:, bsl] = (x1 * sin + x2 * cos).astype(BF16)
        yh = _dot_nt(wkt_ref[0, h * dk:(h + 1) * dk, :], hn)
        y1, y2 = yh[:half], yh[half:]
        kt_ref[0, a, :] = ((y1 * cos_t - y2 * sin_t) * (dk ** -0.5)).astype(BF16)
        kt_ref[0, bsl, :] = ((y1 * sin_t + y2 * cos_t) * (dk ** -0.5)).astype(BF16)
    for c0 in range(0, 2 * d, 4 * LANES):
        v_ref[:, c0:c0 + 4 * LANES] = _dot(hn, w_ref[0, :, 2 * d + c0:2 * d + c0 + 4 * LANES]).astype(BF16)
        gate_ref[:, c0:c0 + 4 * LANES] = _silu(_dot(hn, w_ref[0, :, 4 * d + c0:4 * d + c0 + 4 * LANES]))


def _ret_project(j, x, g_pre, w_in, w_kt, seq):
    t, d = x.shape
    tm = TOKEN_TILE
    b = t // seq
    tiles_per_seq = seq // tm
    half = d // RET_HEADS // 2
    pos = jnp.arange(seq, dtype=F32)
    inv_freq = ROPE_BASE ** (-jnp.arange(half, dtype=F32) / half)
    ang = pos[:, None] * inv_freq[None, :]
    cos, sin = jnp.cos(ang), jnp.sin(ang)
    row = lambda i: (i, 0)
    tab = pl.BlockSpec((tm, half), lambda i: (i % tiles_per_seq, 0))
    tab_t = pl.BlockSpec((half, tm), lambda i: (0, i % tiles_per_seq))
    return pl.pallas_call(
        _ret_proj_body,
        grid=(t // tm,),
        in_specs=[
            pl.BlockSpec((tm, d), row),
            pl.BlockSpec((1, d), lambda i: (0, 0)),
            _const_spec((1, d, 6 * d), lambda i: (j, 0, 0)),
            _const_spec((1, d, d), lambda i: (j, 0, 0)),
            tab, tab, tab_t, tab_t,
        ],
        out_specs=[
            pl.BlockSpec((tm, d), row),
            pl.BlockSpec((1, d, tm), lambda i: (i // tiles_per_seq, 0, i % tiles_per_seq)),
            pl.BlockSpec((tm, 2 * d), row),
            pl.BlockSpec((tm, 2 * d), row),
        ],
        out_shape=[
            jax.ShapeDtypeStruct((t, d), BF16),
            jax.ShapeDtypeStruct((b, d, seq), BF16),
            jax.ShapeDtypeStruct((t, 2 * d), BF16),
            jax.ShapeDtypeStruct((t, 2 * d), F32),
        ],
        compiler_params=_params("arbitrary"),
        name="ret_project",
    )(x, g_pre, w_in, w_kt, cos, sin, cos.T, sin.T)


def _ret_body(q_ref, kt_ref, v_ref, gate_ref, gn_ref, intra_ref, qdec_ref, kdec_ref, cdec_ref,
              o_ref, state):
    @pl.when(pl.program_id(2) == 0)
    def _():
        state[...] = jnp.zeros_like(state)

    per_chunk = RET_CHUNK // RET_REF_CHUNK
    for sub in range(RET_STEP // RET_CHUNK):
        rows = slice(sub * RET_CHUNK, (sub + 1) * RET_CHUNK)
        q = q_ref[0, rows, :]
        kt = kt_ref[0, :, rows]
        v = v_ref[0, rows, :]
        scores = (_dot(q, kt) * intra_ref[0]).astype(BF16)
        o = _dot(scores, v) + _dot(q, state[...].astype(BF16)) * qdec_ref[0]
        kt_dec = (kt.astype(F32) * kdec_ref[0]).astype(BF16)
        state[...] = state[...] * cdec_ref[0] + _dot(kt_dec, v)

        mu = jnp.mean(o, axis=-1, keepdims=True)
        oc = o - mu
        var = jnp.mean(oc * oc, axis=-1, keepdims=True)
        on = oc * lax.rsqrt(var + GN_EPS) * gn_ref[...]
        for cl in range(per_chunk):
            c = sub * per_chunk + cl
            o_ref[0, :, c, :] = gate_ref[0, :, c, :] * on[cl * RET_REF_CHUNK:(cl + 1) * RET_REF_CHUNK, :]


def _ret_decay_tables(chunk):
    h = jnp.arange(RET_HEADS, dtype=F32)
    log_gamma = jnp.log1p(-jnp.exp2(-5.0 - h))
    n = jnp.arange(chunk, dtype=F32)
    diff = n[:, None] - n[None, :]
    intra = jnp.where(diff >= 0, jnp.exp(jnp.maximum(diff, 0.0) * log_gamma[:, None, None]), 0.0)
    qdec = jnp.exp((n + 1.0)[None, :] * log_gamma[:, None])[..., None]
    kdec = jnp.exp((chunk - 1.0 - n)[None, :] * log_gamma[:, None])[:, None]
    cdec = jnp.exp(chunk * log_gamma)[:, None, None]
    return intra, qdec, kdec, cdec


def _retention(q, kt, v, gate, gn):
    b, s, d = q.shape
    c = RET_CHUNK
    step = RET_STEP
    dk = d // RET_HEADS
    dv = 2 * d // RET_HEADS
    n_ref_chunks = s // RET_REF_CHUNK
    ref_chunks_per_step = step // RET_REF_CHUNK
    intra, qdec, kdec, cdec = _ret_decay_tables(c)
    head = lambda bi, h, ci: (h, 0, 0)
    permuted = pl.BlockSpec((1, RET_REF_CHUNK, ref_chunks_per_step, dv), lambda bi, h, ci: (bi, 0, ci, h))
    out = pl.pallas_call(
        _ret_body,
        grid=(b, RET_HEADS, s // step),
        in_specs=[
            pl.BlockSpec((1, step, dk), lambda bi, h, ci: (bi, ci, h)),
            pl.BlockSpec((1, dk, step), lambda bi, h, ci: (bi, h, ci)),
            pl.BlockSpec((1, step, dv), lambda bi, h, ci: (bi, ci, h)),
            permuted,
            pl.BlockSpec((1, dv), lambda bi, h, ci: (0, h)),
            pl.BlockSpec((1, c, c), head),
            pl.BlockSpec((1, c, 1), head),
            pl.BlockSpec((1, 1, c), head),
            pl.BlockSpec((1, 1, 1), head),
        ],
        out_specs=permuted,
        out_shape=jax.ShapeDtypeStruct((b, RET_REF_CHUNK, n_ref_chunks, 2 * d), F32),
        scratch_shapes=[pltpu.VMEM((dk, dv), F32)],
        compiler_params=_params("arbitrary", "arbitrary", "arbitrary"),
        name="retention",
    )(q, kt, v, gate.reshape(b, RET_REF_CHUNK, n_ref_chunks, 2 * d), gn, intra, qdec, kdec, cdec)
    return out.reshape(b * s, 2 * d)


def kernel(x, mem, norm_mix_pre, norm_mix_post, norm_xa_pre, norm_xa_post, norm_mem, norm_ffn_pre, norm_ffn_post, pool_w, pool_scale, sb_w_in, sb_w_out, ret_w_in, ret_gn, ret_w_out, xa_w_q, xa_w_kv, xa_w_o, ffn_w_in, ffn_w_out):
    b, s, d = x.shape
    depth = norm_mix_pre.shape[0]
    t = b * s
    assert s % TOKEN_TILE == 0 and s % RET_STEP == 0 and s % SB_BLOCK == 0

    bf = lambda w: w.astype(BF16)
    pool_w, sb_w_in, sb_w_out, ret_w_in, ret_w_out = map(bf, (pool_w, sb_w_in, sb_w_out, ret_w_in, ret_w_out))
    xa_w_q, xa_w_kv, xa_w_o, ffn_w_in, ffn_w_out = map(bf, (xa_w_q, xa_w_kv, xa_w_o, ffn_w_in, ffn_w_out))
    ret_w_kt = jnp.swapaxes(ret_w_in[:, :, d:2 * d], 1, 2)
    gain3 = lambda g: g.reshape(depth, 1, d)
    post_gains = tuple(map(gain3, (norm_mix_post, norm_xa_pre, norm_xa_post, norm_ffn_pre, norm_ffn_post)))

    kmem, vmem = _memory_kv(mem, norm_mem, xa_w_kv)

    xf = x.reshape(t, d)
    for i in range(depth):
        kind, j = i % N_MIXERS, i // N_MIXERS
        g_pre = norm_mix_pre[i].reshape(1, d)
        w_mix = None
        if kind == 0:
            m = _pool_mixer(j, xf.reshape(b, s, d), g_pre, pool_w, pool_scale.reshape(-1, 1, d)).reshape(t, d)
        elif kind == 1:
            q, k, v = _sb_project(j, xf, g_pre, sb_w_in)
            shape3 = lambda a: a.reshape(b, s, d)
            m = _sb_attention(shape3(q), shape3(k), shape3(v)).reshape(t, d)
            w_mix = (j, sb_w_out)
        else:
            q, kt, v, gate = _ret_project(j, xf, g_pre, ret_w_in, ret_w_kt, s)
            m = _retention(q.reshape(b, s, d), kt, v.reshape(b, s, 2 * d), gate, ret_gn[j].reshape(1, 2 * d))
            w_mix = (j, ret_w_out)
        xf = _post_layer(i, xf, m, w_mix, post_gains, kmem, vmem, xa_w_q, xa_w_o, ffn_w_in, ffn_w_out, s)
    return xf.reshape(b, s, d)
```
